```python
import math
import jax
import jax.numpy as jnp
from jax import lax
import numpy as np

D_MODEL = 2048
BATCH = 4
SEQ = 4096
DEPTH = 4

N_MIXERS = 3
ROPE_THETA = 500000.0
ROPE_FRACTION = 4
Q_BLOCK = 128
NORM_EPS = 1e-6
ADA_CHUNKS = 6
FFN_HIDDEN = -(-8 * D_MODEL // (3 * 256)) * 256

DA_HEADS = D_MODEL // 128
DA_QK_DIM = 64
DA_V_DIM = 2 * DA_QK_DIM
DA_Q_COLS = DA_HEADS * 2 * DA_QK_DIM
DA_IN = 2 * DA_Q_COLS + DA_HEADS * DA_V_DIM

SSD_INNER = 2 * D_MODEL
SSD_HEAD_DIM = 64
SSD_HEADS = SSD_INNER // SSD_HEAD_DIM
SSD_GROUPS = 8
SSD_HPG = SSD_HEADS // SSD_GROUPS
SSD_STATE = 128
SSD_CONV = 4
SSD_CHUNK = 128
SSD_CONV_DIM = SSD_INNER + 2 * SSD_GROUPS * SSD_STATE
SSD_IN = SSD_INNER + SSD_CONV_DIM + SSD_HEADS

SA_HEADS = D_MODEL // 128
SA_KV_HEADS = 4
SA_HEAD_DIM = 128
SA_REP = SA_HEADS // SA_KV_HEADS
IDX_HEADS = 16
IDX_DIM = 64
TOPK_MAX = 256
SA_IN = SA_HEADS * SA_HEAD_DIM + 2 * SA_KV_HEADS * SA_HEAD_DIM + IDX_HEADS * IDX_DIM + IDX_DIM + IDX_HEADS

kernel_name = 'hybrid_diffattn_ssd_dsa_block'


def rms_norm(x, g):
    xf = x.astype(jnp.float32)
    y = xf * lax.rsqrt(jnp.mean(xf * xf, axis=-1, keepdims=True) + NORM_EPS)
    return (y * g.astype(jnp.float32)).astype(x.dtype)


def partial_rope(x, positions):
    head_dim = x.shape[-1]
    rot = head_dim // ROPE_FRACTION
    half = rot // 2
    inv_freq = jnp.power(jnp.float32(ROPE_THETA), -jnp.arange(half, dtype=jnp.float32) / half)
    ang = positions.astype(jnp.float32)[..., None] * inv_freq
    ang = ang.reshape(ang.shape[:2] + (1,) * (x.ndim - 3) + (half,))
    cos = jnp.cos(ang).astype(x.dtype)
    sin = jnp.sin(ang).astype(x.dtype)
    x1, x2, rest = x[..., :half], x[..., half:rot], x[..., rot:]
    return jnp.concatenate([x1 * cos - x2 * sin, x2 * cos + x1 * sin, rest], axis=-1)


def to_blocks(t, n_blocks, block):
    return jnp.moveaxis(t.reshape((t.shape[0], n_blocks, block) + t.shape[2:]), 1, 0)


def from_blocks(t):
    t = jnp.moveaxis(t, 0, 1)
    return t.reshape((t.shape[0], t.shape[1] * t.shape[2]) + t.shape[3:])


def diff_attention(h, positions, w_in, w_out, q_norm_g, k_norm_g, lq1, lk1, lq2, lk2, subln_g, lambda_init):
    bsz, seq, _ = h.shape
    proj = h @ w_in
    q, k, v = jnp.split(proj, [DA_Q_COLS, 2 * DA_Q_COLS], axis=-1)
    q = partial_rope(rms_norm(q.reshape(bsz, seq, DA_HEADS, 2, DA_QK_DIM), q_norm_g), positions)
    k = partial_rope(rms_norm(k.reshape(bsz, seq, DA_HEADS, 2, DA_QK_DIM), k_norm_g), positions)
    v = v.reshape(bsz, seq, DA_HEADS, DA_V_DIM)
    lam = (jnp.exp(jnp.sum(lq1.astype(jnp.float32) * lk1.astype(jnp.float32)))
           - jnp.exp(jnp.sum(lq2.astype(jnp.float32) * lk2.astype(jnp.float32))) + lambda_init)
    scale = DA_QK_DIM ** -0.5
    nb = seq // Q_BLOCK
    k_pos = jnp.arange(seq)

    def block(args):
        qb, start = args
        q_pos = start + jnp.arange(Q_BLOCK)
        s = jnp.einsum('bqhcd,bkhcd->bhcqk', qb, k).astype(jnp.float32) * scale
        s = jnp.where(k_pos[None, :] <= q_pos[:, None], s, -jnp.inf)
        p = jax.nn.softmax(s, axis=-1)
        a = p[:, :, 0] - lam * p[:, :, 1]
        return jnp.einsum('bhqk,bkhd->bqhd', a.astype(v.dtype), v)

    o = from_blocks(lax.map(block, (to_blocks(q, nb, Q_BLOCK), jnp.arange(nb) * Q_BLOCK)))
    o = rms_norm(o, subln_g) * (1.0 - lambda_init)
    return o.reshape(bsz, seq, DA_HEADS * DA_V_DIM) @ w_out


def ssd_chunked_scan(xs, dt, a, bm, cm):
    bsz, seq = xs.shape[:2]
    nc = seq // SSD_CHUNK
    causal = jnp.tril(jnp.ones((SSD_CHUNK, SSD_CHUNK), dtype=bool))

    def step(state, inp):
        xc, dtc, bc, cc = inp
        acum = jnp.cumsum(dtc * a, axis=1)
        seg = acum[:, :, None] - acum[:, None, :]
        decay = jnp.exp(jnp.where(causal[None, :, :, None, None], seg, -jnp.inf))
        xdt = xc * dtc[..., None]
        cb = jnp.einsum('btgn,bsgn->btsg', cc, bc)
        y_intra = jnp.einsum('btsg,btsgh,bsghp->btghp', cb, decay, xdt)
        y_inter = jnp.einsum('btgn,bghpn->btghp', cc, state) * jnp.exp(acum)[..., None]
        to_end = jnp.exp(acum[:, -1:] - acum)
        new_state = (state * jnp.exp(acum[:, -1])[..., None, None]
                     + jnp.einsum('bsgh,bsghp,bsgn->bghpn', to_end, xdt, bc))
        return new_state, y_intra + y_inter

    state0 = jnp.zeros((bsz, SSD_GROUPS, SSD_HPG, SSD_HEAD_DIM, SSD_STATE), jnp.float32)
    inputs = (to_blocks(xs, nc, SSD_CHUNK), to_blocks(dt, nc, SSD_CHUNK),
              to_blocks(bm, nc, SSD_CHUNK), to_blocks(cm, nc, SSD_CHUNK))
    _, ys = lax.scan(step, state0, inputs)
    return from_blocks(ys)


def mamba2_ssd(h, w_in, conv_w, conv_b, dt_bias, a_log, d_skip, norm_g, w_out):
    bsz, seq, _ = h.shape
    proj = h @ w_in
    z, xbc, dt = jnp.split(proj, [SSD_INNER, SSD_INNER + SSD_CONV_DIM], axis=-1)
    xbc = lax.conv_general_dilated(xbc, conv_w[:, None, :].astype(xbc.dtype), window_strides=(1,),
                                   padding=[(SSD_CONV - 1, 0)], dimension_numbers=('NWC', 'WIO', 'NWC'),
                                   feature_group_count=SSD_CONV_DIM) + conv_b
    xbc = jax.nn.silu(xbc)
    xs, bm, cm = jnp.split(xbc, [SSD_INNER, SSD_INNER + SSD_GROUPS * SSD_STATE], axis=-1)
    xs = xs.reshape(bsz, seq, SSD_GROUPS, SSD_HPG, SSD_HEAD_DIM).astype(jnp.float32)
    bm = bm.reshape(bsz, seq, SSD_GROUPS, SSD_STATE).astype(jnp.float32)
    cm = cm.reshape(bsz, seq, SSD_GROUPS, SSD_STATE).astype(jnp.float32)
    dt = jax.nn.softplus(dt.astype(jnp.float32) + dt_bias.astype(jnp.float32))
    dt = dt.reshape(bsz, seq, SSD_GROUPS, SSD_HPG)
    a = -jnp.exp(a_log.astype(jnp.float32)).reshape(SSD_GROUPS, SSD_HPG)
    y = ssd_chunked_scan(xs, dt, a, bm, cm)
    y = y + d_skip.astype(jnp.float32).reshape(SSD_GROUPS, SSD_HPG)[:, :, None] * xs
    y = y.reshape(bsz, seq, SSD_GROUPS, SSD_HPG * SSD_HEAD_DIM).astype(h.dtype)
    y = y * jax.nn.silu(z).reshape(bsz, seq, SSD_GROUPS, SSD_HPG * SSD_HEAD_DIM)
    y = rms_norm(y, norm_g.reshape(SSD_GROUPS, SSD_HPG * SSD_HEAD_DIM))
    return y.reshape(bsz, seq, SSD_INNER) @ w_out


def dsa_attention(h, positions, w_in, w_out, q_norm_g, k_norm_g, idx_k_norm_g):
    bsz, seq, _ = h.shape
    proj = h @ w_in
    o1 = SA_HEADS * SA_HEAD_DIM
    o2 = o1 + SA_KV_HEADS * SA_HEAD_DIM
    o3 = o2 + SA_KV_HEADS * SA_HEAD_DIM
    o4 = o3 + IDX_HEADS * IDX_DIM
    o5 = o4 + IDX_DIM
    q, k, v, qi, ki, wi = jnp.split(proj, [o1, o2, o3, o4, o5], axis=-1)
    q = partial_rope(rms_norm(q.reshape(bsz, seq, SA_HEADS, SA_HEAD_DIM), q_norm_g), positions)
    q = q.reshape(bsz, seq, SA_KV_HEADS, SA_REP, SA_HEAD_DIM)
    k = partial_rope(rms_norm(k.reshape(bsz, seq, SA_KV_HEADS, SA_HEAD_DIM), k_norm_g), positions)
    v = v.reshape(bsz, seq, SA_KV_HEADS, SA_HEAD_DIM)
    qi = partial_rope(qi.reshape(bsz, seq, IDX_HEADS, IDX_DIM), positions)
    ki = partial_rope(rms_norm(ki, idx_k_norm_g), positions)
    wi = wi * (IDX_HEADS ** -0.5)
    k_sel = min(TOPK_MAX, seq // 4)
    nb = seq // Q_BLOCK
    k_pos = jnp.arange(seq)
    b_idx = jnp.arange(bsz)[:, None, None]

    def block(args):
        qb, qib, wib, start = args
        q_pos = start + jnp.arange(Q_BLOCK)
        rel = jax.nn.relu(jnp.einsum('bqhd,bkd->bqhk', qib, ki).astype(jnp.float32) * IDX_DIM ** -0.5)
        score = jnp.einsum('bqh,bqhk->bqk', wib.astype(jnp.float32), rel)
        score = jnp.where((k_pos[None, :] <= q_pos[:, None])[None], score, -jnp.inf)
        _, idx = lax.top_k(score, k_sel)
        ks = k[b_idx, idx]
        vs = v[b_idx, idx]
        s = jnp.einsum('bqgrd,bqkgd->bqgrk', qb, ks).astype(jnp.float32) * SA_HEAD_DIM ** -0.5
        valid = idx <= q_pos[None, :, None]
        s = jnp.where(valid[:, :, None, None, :], s, -jnp.inf)
        p = jax.nn.softmax(s, axis=-1)
        return jnp.einsum('bqgrk,bqkgd->bqgrd', p.astype(vs.dtype), vs)

    o = lax.map(block, (to_blocks(q, nb, Q_BLOCK), to_blocks(qi, nb, Q_BLOCK),
                        to_blocks(wi, nb, Q_BLOCK), jnp.arange(nb) * Q_BLOCK))
    o = from_blocks(o).reshape(bsz, seq, SA_HEADS * SA_HEAD_DIM)
    return o @ w_out


def swiglu(h, w_gate_up, w_down):
    g, u = jnp.split(h @ w_gate_up, 2, axis=-1)
    return (jax.nn.silu(g) * u) @ w_down


def setup_inputs(seed: int = 0) -> dict:
    key = jax.random.key(seed)
    ks = list(jax.random.split(key, 40))
    n_a, n_b, n_c = (len(range(m, DEPTH, N_MIXERS)) for m in range(N_MIXERS))

    def nrm(k, shape, scale):
        return jax.random.normal(k, shape, jnp.float32) * scale

    def gain(k, shape):
        return 1.0 + nrm(k, shape, 0.1)

    dt0 = jnp.exp(jax.random.uniform(ks[24], (n_b, SSD_HEADS), jnp.float32,
                                     minval=math.log(1e-3), maxval=math.log(1e-1)))
    return {
        'x': nrm(ks[0], (BATCH, SEQ, D_MODEL), 1.0),
        'c': nrm(ks[1], (BATCH, D_MODEL), 1.0),
        'positions': (jax.random.randint(ks[2], (BATCH, 1), 0, 1024, dtype=jnp.int32)
                      + jnp.arange(SEQ, dtype=jnp.int32)[None, :]),
        'norm1_g': gain(ks[3], (DEPTH, D_MODEL)),
        'norm2_g': gain(ks[4], (DEPTH, D_MODEL)),
        'ada_w': nrm(ks[5], (DEPTH, D_MODEL, ADA_CHUNKS * D_MODEL), 0.3 * D_MODEL ** -0.5),
        'ada_b': nrm(ks[6], (DEPTH, ADA_CHUNKS * D_MODEL), 0.02),
        'ffn_w_gate_up': nrm(ks[7], (DEPTH, D_MODEL, 2 * FFN_HIDDEN), D_MODEL ** -0.5),
        'ffn_w_down': nrm(ks[8], (DEPTH, FFN_HIDDEN, D_MODEL), FFN_HIDDEN ** -0.5),
        'da_w_in': nrm(ks[9], (n_a, D_MODEL, DA_IN), D_MODEL ** -0.5),
        'da_w_out': nrm(ks[10], (n_a, DA_HEADS * DA_V_DIM, D_MODEL), (DA_HEADS * DA_V_DIM) ** -0.5),
        'da_q_norm_g': gain(ks[11], (n_a, DA_QK_DIM)),
        'da_k_norm_g': gain(ks[12], (n_a, DA_QK_DIM)),
        'da_lambda_q1': nrm(ks[13], (n_a, DA_QK_DIM), 0.1),
        'da_lambda_k1': nrm(ks[14], (n_a, DA_QK_DIM), 0.1),
        'da_lambda_q2': nrm(ks[15], (n_a, DA_QK_DIM), 0.1),
        'da_lambda_k2': nrm(ks[16], (n_a, DA_QK_DIM), 0.1),
        'da_subln_g': gain(ks[17], (n_a, DA_V_DIM)),
        'ssd_w_in': nrm(ks[18], (n_b, D_MODEL, SSD_IN), D_MODEL ** -0.5),
        'ssd_conv_w': nrm(ks[19], (n_b, SSD_CONV, SSD_CONV_DIM), SSD_CONV ** -0.5),
        'ssd_conv_b': nrm(ks[20], (n_b, SSD_CONV_DIM), 0.02),
        'ssd_dt_bias': dt0 + jnp.log(-jnp.expm1(-dt0)),
        'ssd_a_log': jnp.log(jax.random.uniform(ks[21], (n_b, SSD_HEADS), jnp.float32, minval=1.0, maxval=16.0)),
        'ssd_d_skip': gain(ks[22], (n_b, SSD_HEADS)),
        'ssd_norm_g': gain(ks[23], (n_b, SSD_INNER)),
        'ssd_w_out': nrm(ks[25], (n_b, SSD_INNER, D_MODEL), SSD_INNER ** -0.5),
        'sa_w_in': nrm(ks[26], (n_c, D_MODEL, SA_IN), D_MODEL ** -0.5),
        'sa_w_out': nrm(ks[27], (n_c, SA_HEADS * SA_HEAD_DIM, D_MODEL), (SA_HEADS * SA_HEAD_DIM) ** -0.5),
        'sa_q_norm_g': gain(ks[28], (n_c, SA_HEAD_DIM)),
        'sa_k_norm_g': gain(ks[29], (n_c, SA_HEAD_DIM)),
        'sa_idx_k_norm_g': gain(ks[30], (n_c, IDX_DIM)),
    }


def reference(x, c, positions, norm1_g, norm2_g, ada_w, ada_b, ffn_w_gate_up, ffn_w_down,
              da_w_in, da_w_out, da_q_norm_g, da_k_norm_g, da_lambda_q1, da_lambda_k1,
              da_lambda_q2, da_lambda_k2, da_subln_g,
              ssd_w_in, ssd_conv_w, ssd_conv_b, ssd_dt_bias, ssd_a_log, ssd_d_skip, ssd_norm_g, ssd_w_out,
              sa_w_in, sa_w_out, sa_q_norm_g, sa_k_norm_g, sa_idx_k_norm_g):
    c_act = jax.nn.silu(c)
    for i in range(DEPTH):
        kind, j = i % N_MIXERS, i // N_MIXERS
        ada = c_act @ ada_w[i] + ada_b[i]
        shift1, scale1, gate1, shift2, scale2, gate2 = jnp.split(ada[:, None, :], ADA_CHUNKS, axis=-1)
        h = rms_norm(x, norm1_g[i]) * (1.0 + scale1) + shift1
        if kind == 0:
            lambda_init = 0.8 - 0.6 * math.exp(-0.3 * i)
            m = diff_attention(h, positions, da_w_in[j], da_w_out[j], da_q_norm_g[j], da_k_norm_g[j],
                               da_lambda_q1[j], da_lambda_k1[j], da_lambda_q2[j], da_lambda_k2[j],
                               da_subln_g[j], lambda_init)
        elif kind == 1:
            m = mamba2_ssd(h, ssd_w_in[j], ssd_conv_w[j], ssd_conv_b[j], ssd_dt_bias[j], ssd_a_log[j],
                           ssd_d_skip[j], ssd_norm_g[j], ssd_w_out[j])
        else:
            m = dsa_attention(h, positions, sa_w_in[j], sa_w_out[j], sa_q_norm_g[j], sa_k_norm_g[j],
                              sa_idx_k_norm_g[j])
        x = x + gate1 * m
        h = rms_norm(x, norm2_g[i]) * (1.0 + scale2) + shift2
        x = x + gate2 * swiglu(h, ffn_w_gate_up[i], ffn_w_down[i])
    return x
```

```python
import functools
import math

import jax
import jax.numpy as jnp
from jax import lax
from jax.experimental import pallas as pl
from jax.experimental.pallas import tpu as pltpu

F32 = jnp.float32
BF16 = jnp.bfloat16
I32 = jnp.int32

D_MODEL = 2048
N_MIXERS = 3
ROPE_THETA = 500000.0
NORM_EPS = 1e-6
ADA_CHUNKS = 6
FFN_HIDDEN = 5632
LANES = 128
NEG_BIG = -1e30
INT_MIN = -2 ** 31

DA_HEADS = 16
DA_QK_DIM = 64
DA_Q_COLS = 2048
DA_IN = 6144

SSD_INNER = 4096
SSD_HEAD_DIM = 64
SSD_HEADS = 64
SSD_GROUPS = 8
SSD_HPG = 8
SSD_STATE = 128
SSD_CONV = 4
SSD_CHUNK = 128
SSD_CONV_DIM = 6144
SSD_GW = SSD_HPG * SSD_HEAD_DIM
SSD_DT_OFF = SSD_INNER + SSD_CONV_DIM
SSD_PROJ = SSD_DT_OFF + SSD_GROUPS * LANES

SA_HEADS = 16
SA_KV_HEADS = 4
SA_HEAD_DIM = 128
SA_REP = 4
IDX_HEADS = 16
IDX_DIM = 64
TOPK = 256
SA_IN = 4176
SA_PROJ = 4224
SA_QI_OFF = 3072
SA_KI_OFF = 4096

VMEM_LIMIT = 56 * 1024 * 1024


def _cp(*sem):
    return pltpu.CompilerParams(dimension_semantics=sem, vmem_limit_bytes=VMEM_LIMIT)


def _nt(a, b):
    return lax.dot_general(a, b, (((1,), (1,)), ((), ())), preferred_element_type=F32)


def _dot(a, b):
    return jnp.dot(a, b, preferred_element_type=F32)


def _silu(x):
    return x * (1.0 / (1.0 + jnp.exp(-x)))


def _rope_table_kernel(pos_ref, pat_ref, c_ref, s1_ref, s2_ref):
    ang = pos_ref[...].astype(F32) * pat_ref[0:1, :]
    s = jnp.sin(ang)
    c_ref[...] = jnp.cos(ang)
    s1_ref[...] = s * pat_ref[1:2, :]
    s2_ref[...] = s * pat_ref[2:3, :]


def _rope_pattern(head_dim):
    rot = head_dim // 4
    half = rot // 2
    lane = jnp.arange(LANES) % head_dim
    inv_freq = jnp.power(jnp.float32(ROPE_THETA), -(lane % half).astype(F32) / half)
    freq = jnp.where(lane < rot, inv_freq, 0.0)
    m1 = jnp.where(lane < half, -1.0, 0.0)
    m2 = jnp.where((lane >= half) & (lane < rot), 1.0, 0.0)
    pat = jnp.zeros((8, LANES), F32)
    return pat.at[0].set(freq).at[1].set(m1).at[2].set(m2)


def _rope_tables(pos_col, head_dim):
    m = pos_col.shape[0]
    tm = min(m, 1024)
    out = jax.ShapeDtypeStruct((m, LANES), F32)
    return pl.pallas_call(
        _rope_table_kernel,
        grid=(m // tm,),
        in_specs=[pl.BlockSpec((tm, 1), lambda i: (i, 0)), pl.BlockSpec((8, LANES), lambda i: (0, 0))],
        out_specs=[pl.BlockSpec((tm, LANES), lambda i: (i, 0))] * 3,
        out_shape=[out] * 3,
        compiler_params=_cp("parallel"),
        name="rope_tables",
    )(pos_col, _rope_pattern(head_dim))


def _rope(x, c, s1, s2, half):
    return x * c + pltpu.roll(x, LANES - half, 1) * s1 + pltpu.roll(x, half, 1) * s2


def _ada_kernel(c_ref, w_ref, b_ref, o_ref):
    c = c_ref[...]
    ca = _silu(c).astype(BF16)
    o_ref[...] = _dot(ca, w_ref[...].astype(BF16)) + b_ref[...]


def _ada(c, ada_w, ada_b):
    depth, d, n = ada_w.shape
    bsz = c.shape[0]
    tn = 1024
    return pl.pallas_call(
        _ada_kernel,
        grid=(depth, n // tn),
        in_specs=[pl.BlockSpec((bsz, d), lambda l, j: (0, 0)),
                  pl.BlockSpec((None, d, tn), lambda l, j: (l, 0, j)),
                  pl.BlockSpec((None, 1, tn), lambda l, j: (l, 0, j))],
        out_specs=pl.BlockSpec((None, bsz, tn), lambda l, j: (l, 0, j)),
        out_shape=jax.ShapeDtypeStruct((depth, bsz, n), F32),
        compiler_params=_cp("parallel", "parallel"),
        name="ada",
    )(c, ada_w, ada_b.reshape(depth, 1, n))


def _modulated_norm(x, g, sc, sh):
    ms = jnp.mean(x * x, axis=-1, keepdims=True)
    return (x * lax.rsqrt(ms + NORM_EPS) * g) * (1.0 + sc) + sh


def _norm_proj_kernel(x_ref, g_ref, sc_ref, sh_ref, w_ref, o_ref, h_ref):
    @pl.when(pl.program_id(1) == 0)
    def _():
        h_ref[...] = _modulated_norm(x_ref[...], g_ref[...], sc_ref[...], sh_ref[...]).astype(BF16)

    o_ref[...] = _dot(h_ref[...], w_ref[...]).astype(o_ref.dtype)


def _mod_specs(tm, seq, d):
    row = lambda i, j: (i, 0)
    per_batch = lambda i, j: ((i * tm) // seq, 0, 0)
    return [pl.BlockSpec((tm, d), row),
            pl.BlockSpec((1, d), lambda i, j: (0, 0)),
            pl.BlockSpec((None, 1, d), per_batch),
            pl.BlockSpec((None, 1, d), per_batch)]


def _norm_proj(x, g, scale, shift, w, seq, tn, out_dtype=F32, tm=512):
    m, d = x.shape
    n = w.shape[1]
    tm = min(tm, seq)
    return pl.pallas_call(
        _norm_proj_kernel,
        grid=(m // tm, n // tn),
        in_specs=_mod_specs(tm, seq, d) + [pl.BlockSpec((d, tn), lambda i, j: (0, j))],
        out_specs=pl.BlockSpec((tm, tn), lambda i, j: (i, j)),
        out_shape=jax.ShapeDtypeStruct((m, n), out_dtype),
        scratch_shapes=[pltpu.VMEM((tm, d), BF16)],
        compiler_params=_cp("parallel", "arbitrary"),
        name="norm_proj",
    )(x, g.reshape(1, d), scale, shift, w)


def _norm_ffn_up_kernel(x_ref, g_ref, sc_ref, sh_ref, wg_ref, wu_ref, o_ref, h_ref):
    @pl.when(pl.program_id(1) == 0)
    def _():
        h_ref[...] = _modulated_norm(x_ref[...], g_ref[...], sc_ref[...], sh_ref[...]).astype(BF16)

    h = h_ref[...]
    gate = _dot(h, wg_ref[...])
    up = _dot(h, wu_ref[...])
    o_ref[...] = (_silu(gate) * up).astype(o_ref.dtype)


def _norm_ffn_up(x, g, scale, shift, w_gate_up, seq, tn=512, tm=512):
    m, d = x.shape
    hid = w_gate_up.shape[1] // 2
    nj = hid // tn
    tm = min(tm, seq)
    return pl.pallas_call(
        _norm_ffn_up_kernel,
        grid=(m // tm, nj),
        in_specs=_mod_specs(tm, seq, d) + [pl.BlockSpec((d, tn), lambda i, j: (0, j)),
                                           pl.BlockSpec((d, tn), lambda i, j: (0, j + nj))],
        out_specs=pl.BlockSpec((tm, tn), lambda i, j: (i, j)),
        out_shape=jax.ShapeDtypeStruct((m, hid), BF16),
        scratch_shapes=[pltpu.VMEM((tm, d), BF16)],
        compiler_params=_cp("parallel", "arbitrary"),
        name="norm_ffn_up",
    )(x, g.reshape(1, d), scale, shift, w_gate_up, w_gate_up)


def _res_matmul_kernel(a_ref, w_ref, x_ref, gate_ref, o_ref):
    o_ref[...] = x_ref[...] + gate_ref[...] * _dot(a_ref[...], w_ref[...])


def _res_matmul(a, w, x, gate, seq, tn=512, tm=512):
    m, k = a.shape
    d = w.shape[1]
    tm = min(tm, seq)
    return pl.pallas_call(
        _res_matmul_kernel,
        grid=(m // tm, d // tn),
        in_specs=[pl.BlockSpec((tm, k), lambda i, j: (i, 0)),
                  pl.BlockSpec((k, tn), lambda i, j: (0, j)),
                  pl.BlockSpec((tm, tn), lambda i, j: (i, j)),
                  pl.BlockSpec((None, 1, tn), lambda i, j: ((i * tm) // seq, 0, j))],
        out_specs=pl.BlockSpec((tm, tn), lambda i, j: (i, j)),
        out_shape=jax.ShapeDtypeStruct((m, d), F32),
        compiler_params=_cp("parallel", "parallel"),
        name="res_matmul",
    )(a, w, x, gate)


def _da_prep_kernel(p_ref, c_ref, s1_ref, s2_ref, gq_ref, gk_ref, o_ref):
    c, s1, s2 = c_ref[...], s1_ref[...], s2_ref[...]
    tm = p_ref.shape[0]
    lo = lax.broadcasted_iota(I32, (tm, LANES), 1) < DA_QK_DIM
    n_qk = DA_Q_COLS // LANES
    for blk in range(2 * n_qk):
        x = p_ref[:, blk * LANES:(blk + 1) * LANES]
        x2 = x * x
        s_lo = jnp.sum(jnp.where(lo, x2, 0.0), axis=-1, keepdims=True)
        s_all = jnp.sum(x2, axis=-1, keepdims=True)
        ms = jnp.where(lo, s_lo, s_all - s_lo) * (1.0 / DA_QK_DIM)
        gain = gq_ref[...] if blk < n_qk else gk_ref[...]
        y = _rope(x * lax.rsqrt(ms + NORM_EPS) * gain, c, s1, s2, DA_QK_DIM // 8)
        if blk < n_qk:
            y = y * (DA_QK_DIM ** -0.5)
        o_ref[:, blk * LANES:(blk + 1) * LANES] = y.astype(BF16)
    o_ref[:, 2 * DA_Q_COLS:] = p_ref[:, 2 * DA_Q_COLS:].astype(BF16)


def _da_prep(proj, tabs, gq, gk, tm=256):
    m, n = proj.shape
    tm = min(tm, m)
    tab_spec = pl.BlockSpec((tm, LANES), lambda i: (i, 0))
    vec_spec = pl.BlockSpec((1, LANES), lambda i: (0, 0))
    return pl.pallas_call(
        _da_prep_kernel,
        grid=(m // tm,),
        in_specs=[pl.BlockSpec((tm, n), lambda i: (i, 0)), tab_spec, tab_spec, tab_spec, vec_spec, vec_spec],
        out_specs=pl.BlockSpec((tm, n), lambda i: (i, 0)),
        out_shape=jax.ShapeDtypeStruct((m, n), BF16),
        compiler_params=_cp("parallel"),
        name="da_prep",
    )(proj, *tabs, jnp.tile(gq, 2).reshape(1, LANES), jnp.tile(gk, 2).reshape(1, LANES))


def _da_attn_kernel(q_ref, k_ref, v_ref, lam_ref, g_ref, o_ref, m_ref, l_ref, acc_ref, *, tq, lam_init):
    qb = pl.program_id(2)
    q = q_ref[...]
    lane = lax.broadcasted_iota(I32, (tq, LANES), 1)
    zero = jnp.zeros_like(q)
    qs = jnp.concatenate([jnp.where(lane < DA_QK_DIM, q, zero), jnp.where(lane >= DA_QK_DIM, q, zero)], axis=0)
    m_ref[...] = jnp.full(m_ref.shape, NEG_BIG, F32)
    l_ref[...] = jnp.zeros(l_ref.shape, F32)
    acc_ref[...] = jnp.zeros(acc_ref.shape, F32)

    def step(j, masked):
        start = pl.multiple_of(j * tq, tq)
        kb = k_ref[pl.ds(start, tq), :]
        vb = v_ref[pl.ds(start, tq), :]
        s = _nt(qs, kb)
        if masked:
            row = lax.broadcasted_iota(I32, (2 * tq, tq), 0)
            col = lax.broadcasted_iota(I32, (2 * tq, tq), 1)
            row = jnp.where(row >= tq, row - tq, row)
            s = jnp.where(col <= row, s, NEG_BIG)
        m_old = m_ref[...]
        m_new = jnp.maximum(m_old, jnp.max(s, axis=-1, keepdims=True))
        alpha = jnp.exp(m_old - m_new)
        p = jnp.exp(s - m_new)
        l_ref[...] = alpha * l_ref[...] + jnp.sum(p, axis=-1, keepdims=True)
        acc_ref[...] = alpha * acc_ref[...] + _dot(p.astype(BF16), vb)
        m_ref[...] = m_new

    def body(j, carry):
        step(j, False)
        return carry

    lax.fori_loop(0, qb, body, 0)
    step(qb, True)

    lam_p = lam_ref[...]
    lam = (jnp.exp(jnp.sum(lam_p[0:1] * lam_p[1:2], axis=-1, keepdims=True))
           - jnp.exp(jnp.sum(lam_p[2:3] * lam_p[3:4], axis=-1, keepdims=True)) + lam_init)
    o_all = acc_ref[...] / l_ref[...]
    o = o_all[:tq] - lam * o_all[tq:]
    ms = jnp.mean(o * o, axis=-1, keepdims=True)
    o_ref[...] = (o * lax.rsqrt(ms + NORM_EPS) * g_ref[...] * (1.0 - lam_init)).astype(BF16)


def _da_attn(qkv, lam_params, subln_g, bsz, seq, lam_init, tq=256):
    tq = min(tq, seq)
    nq = seq // tq
    nh = DA_HEADS
    kv_spec = lambda off: pl.BlockSpec((seq, LANES), lambda b, h, i: (b, off + h))
    return pl.pallas_call(
        functools.partial(_da_attn_kernel, tq=tq, lam_init=lam_init),
        grid=(bsz, nh, nq),
        in_specs=[pl.BlockSpec((tq, LANES), lambda b, h, i: (b * nq + i, h)),
                  kv_spec(nh), kv_spec(2 * nh),
                  pl.BlockSpec((4, DA_QK_DIM), lambda b, h, i: (0, 0)),
                  pl.BlockSpec((1, LANES), lambda b, h, i: (0, 0))],
        out_specs=pl.BlockSpec((tq, LANES), lambda b, h, i: (b * nq + i, h)),
        out_shape=jax.ShapeDtypeStruct((bsz * seq, nh * LANES), BF16),
        scratch_shapes=[pltpu.VMEM((2 * tq, 1), F32), pltpu.VMEM((2 * tq, 1), F32),
                        pltpu.VMEM((2 * tq, LANES), F32)],
        compiler_params=_cp("parallel", "parallel", "arbitrary"),
        name="da_attn",
    )(qkv, qkv, qkv, lam_params, subln_g.reshape(1, LANES))


def _ssd_conv_kernel(x_ref, halo_ref, w_ref, b_ref, o_ref, xe_ref, *, ts, tiles_per_seq):
    first = (pl.program_id(0) % tiles_per_seq) == 0
    xe_ref[0:8, :] = jnp.where(first, 0.0, halo_ref[...])
    xe_ref[8:, :] = x_ref[...]
    y = b_ref[...] + w_ref[3:4, :] * x_ref[...]
    for tap in range(SSD_CONV - 1):
        y = y + w_ref[tap:tap + 1, :] * xe_ref[pl.ds(5 + tap, ts), :]
    o_ref[...] = _silu(y).astype(BF16)


def _ssd_conv(proj, conv_w, conv_b, seq, ts=512, tc=512):
    m = proj.shape[0]
    ts = min(ts, seq)
    col0 = SSD_INNER // tc
    rb = ts // 8
    return pl.pallas_call(
        functools.partial(_ssd_conv_kernel, ts=ts, tiles_per_seq=seq // ts),
        grid=(m // ts, SSD_CONV_DIM // tc),
        in_specs=[pl.BlockSpec((ts, tc), lambda i, j: (i, col0 + j)),
                  pl.BlockSpec((8, tc), lambda i, j: (jnp.maximum(i * rb - 1, 0), col0 + j)),
                  pl.BlockSpec((SSD_CONV, tc), lambda i, j: (0, j)),
                  pl.BlockSpec((1, tc), lambda i, j: (0, j))],
        out_specs=pl.BlockSpec((ts, tc), lambda i, j: (i, j)),
        out_shape=jax.ShapeDtypeStruct((m, SSD_CONV_DIM), BF16),
        scratch_shapes=[pltpu.VMEM((ts + 8, tc), F32)],
        compiler_params=_cp("parallel", "parallel"),
        name="ssd_conv",
    )(proj, proj, conv_w, conv_b.reshape(1, SSD_CONV_DIM))


def _split_bf16(x, parts):
    out = []
    for _ in range(parts):
        hi = x.astype(BF16)
        out.append(hi)
        x = x - hi.astype(F32)
    return out


def _ssd_scan_kernel(xs_ref, b_ref, c_ref, dt_ref, z_ref, bias_ref, alog_ref, dskip_ref, ng_ref, e_ref,
                     o_ref, state_ref):
    q = SSD_CHUNK

    @pl.when(pl.program_id(2) == 0)
    def _():
        state_ref[...] = jnp.zeros(state_ref.shape, F32)

    row = lax.broadcasted_iota(I32, (q, q), 0)
    col = lax.broadcasted_iota(I32, (q, q), 1)
    causal = col <= row
    lane_lo = col < SSD_HEAD_DIM

    x = dt_ref[...] + bias_ref[...]
    dt = jnp.maximum(x, 0.0) + jnp.log1p(jnp.exp(-jnp.abs(x)))
    dta = dt * (-jnp.exp(alog_ref[...]))
    tri = jnp.where(causal, 1.0, 0.0).astype(BF16)
    parts = _dot(tri, jnp.concatenate(_split_bf16(dta, 3), axis=1))
    acum = parts[:, :q] + parts[:, q:2 * q] + parts[:, 2 * q:]
    a_last = acum[q - 1:q, :]
    e_acum = jnp.exp(acum)
    w_end = dt * jnp.exp(a_last - acum)
    decay = jnp.broadcast_to(jnp.exp(a_last), (8, q))
    stacked = jnp.concatenate([e_acum, w_end, decay], axis=0)
    expanded = _dot(jnp.concatenate(_split_bf16(stacked, 2), axis=1), e_ref[...])
    e_acum_x = expanded[:q]
    w_end_x = expanded[q:2 * q]
    decay_x = expanded[2 * q:2 * q + 1]

    acum_t = acum.T
    dt_t = dt.T
    bm = b_ref[...]
    cm = c_ref[...]
    cb = _nt(cm, bm)
    xs = xs_ref[...]
    xs32 = xs.astype(F32)
    state = state_ref[...]
    y_inter = _dot(cm, state.astype(BF16)) * e_acum_x

    y_pairs = []
    for hp in range(SSD_HPG // 2):
        xs_pair = xs[:, hp * LANES:(hp + 1) * LANES]
        ys = []
        for e in range(2):
            h = 2 * hp + e
            seg = acum[:, h:h + 1] - acum_t[h:h + 1, :]
            decay_ts = jnp.exp(jnp.where(causal, seg, NEG_BIG))
            mh = (cb * decay_ts * dt_t[h:h + 1, :]).astype(BF16)
            ys.append(_dot(mh, xs_pair))
        y_pairs.append(jnp.where(lane_lo, ys[0], ys[1]))
    y = jnp.concatenate(y_pairs, axis=1) + y_inter + dskip_ref[...] * xs32

    gated = y * _silu(z_ref[...])
    ms = jnp.mean(gated * gated, axis=-1, keepdims=True)
    o_ref[...] = (gated * lax.rsqrt(ms + NORM_EPS) * ng_ref[...]).astype(BF16)

    bm_t = bm.astype(F32).T.astype(BF16)
    state_ref[...] = state * decay_x + _dot(bm_t, (xs32 * w_end_x).astype(BF16))


def _ssd_scan(xbc, proj, dt_bias, a_log, d_skip, norm_g, bsz, seq):
    q = SSD_CHUNK
    nc = seq // q
    g_n = SSD_GROUPS
    gw = SSD_GW
    pad = lambda v: jnp.pad(v.reshape(g_n, 1, SSD_HPG), ((0, 0), (0, 0), (0, LANES - SSD_HPG)))
    expand = (jnp.arange(gw)[None, :] // SSD_HEAD_DIM == jnp.arange(LANES)[:, None]).astype(BF16)
    expand2 = jnp.concatenate([expand, expand], axis=0)
    rows = lambda b, g, c: b * nc + c
    grp = lambda b, g, c: (g, 0, 0)
    return pl.pallas_call(
        _ssd_scan_kernel,
        grid=(bsz, g_n, nc),
        in_specs=[pl.BlockSpec((q, gw), lambda b, g, c: (rows(b, g, c), g)),
                  pl.BlockSpec((q, LANES), lambda b, g, c: (rows(b, g, c), SSD_INNER // LANES + g)),
                  pl.BlockSpec((q, LANES), lambda b, g, c: (rows(b, g, c), SSD_INNER // LANES + g_n + g)),
                  pl.BlockSpec((q, LANES), lambda b, g, c: (rows(b, g, c), SSD_DT_OFF // LANES + g)),
                  pl.BlockSpec((q, gw), lambda b, g, c: (rows(b, g, c), g)),
                  pl.BlockSpec((None, 1, LANES), grp),
                  pl.BlockSpec((None, 1, LANES), grp),
                  pl.BlockSpec((None, 1, gw), grp),
                  pl.BlockSpec((None, 1, gw), grp),
                  pl.BlockSpec((2 * LANES, gw), lambda b, g, c: (0, 0))],
        out_specs=pl.BlockSpec((q, gw), lambda b, g, c: (rows(b, g, c), g)),
        out_shape=jax.ShapeDtypeStruct((bsz * seq, SSD_INNER), BF16),
        scratch_shapes=[pltpu.VMEM((SSD_STATE, gw), F32)],
        compiler_params=_cp("parallel", "parallel", "arbitrary"),
        name="ssd_scan",
    )(xbc, xbc, xbc, proj, proj, pad(dt_bias), pad(a_log),
      jnp.repeat(d_skip, SSD_HEAD_DIM).reshape(g_n, 1, gw), norm_g.reshape(g_n, 1, gw), expand2)


def _dsa_prep_kernel(p_ref, c64_ref, s164_ref, s264_ref, c128_ref, s1128_ref, s2128_ref, gq_ref, gk_ref, gi_ref,
                     q_ref, k_ref, vt_ref, qi_ref, ki_ref, wt_ref):
    tm = p_ref.shape[0]
    c64, s164, s264 = c64_ref[...], s164_ref[...], s264_ref[...]
    c128, s1128, s2128 = c128_ref[...], s1128_ref[...], s2128_ref[...]

    def normed(x, gain):
        ms = jnp.mean(x * x, axis=-1, keepdims=True)
        return _rope(x * lax.rsqrt(ms + NORM_EPS) * gain, c128, s1128, s2128, SA_HEAD_DIM // 8)

    for h in range(SA_HEADS):
        x = p_ref[:, h * LANES:(h + 1) * LANES]
        q_ref[:, h * LANES:(h + 1) * LANES] = (normed(x, gq_ref[...]) * (SA_HEAD_DIM ** -0.5)).astype(BF16)
    for g in range(SA_KV_HEADS):
        x = p_ref[:, (SA_HEADS + g) * LANES:(SA_HEADS + g + 1) * LANES]
        k_ref[:, g * LANES:(g + 1) * LANES] = normed(x, gk_ref[...]).astype(BF16)
        v = p_ref[:, (SA_HEADS + SA_KV_HEADS + g) * LANES:(SA_HEADS + SA_KV_HEADS + g + 1) * LANES]
        vt_ref[g * LANES:(g + 1) * LANES, :] = v.T.astype(BF16)
    for blk in range(IDX_HEADS * IDX_DIM // LANES):
        x = p_ref[:, SA_QI_OFF + blk * LANES:SA_QI_OFF + (blk + 1) * LANES]
        qi_ref[:, blk * LANES:(blk + 1) * LANES] = (
            _rope(x, c64, s164, s264, IDX_DIM // 8) * (IDX_DIM ** -0.5)).astype(BF16)
    x = p_ref[:, SA_KI_OFF:SA_KI_OFF + LANES]
    lo = lax.broadcasted_iota(I32, (tm, LANES), 1) < IDX_DIM
    xk = jnp.where(lo, x, 0.0)
    ms = jnp.sum(xk * xk, axis=-1, keepdims=True) * (1.0 / IDX_DIM)
    ki = _rope(xk * lax.rsqrt(ms + NORM_EPS) * gi_ref[...], c64, s164, s264, IDX_DIM // 8)
    ki = jnp.where(lo, ki, 0.0)
    ki_ref[:, :LANES] = ki.astype(BF16)
    ki_ref[:, LANES:] = pltpu.roll(ki, IDX_DIM, 1).astype(BF16)
    wt_ref[...] = x.T * (IDX_HEADS ** -0.5)


def _dsa_prep(proj, tabs64, tabs128, gq, gk, gi, tm=256):
    m = proj.shape[0]
    tm = min(tm, m)
    row = lambda n: pl.BlockSpec((tm, n), lambda i: (i, 0))
    vec = pl.BlockSpec((1, LANES), lambda i: (0, 0))
    kvw = SA_KV_HEADS * SA_HEAD_DIM
    gi_pad = jnp.concatenate([gi, jnp.zeros((LANES - IDX_DIM,), F32)])
    return pl.pallas_call(
        _dsa_prep_kernel,
        grid=(m // tm,),
        in_specs=[row(SA_PROJ)] + [row(LANES)] * 6 + [vec] * 3,
        out_specs=[row(SA_HEADS * SA_HEAD_DIM), row(kvw), pl.BlockSpec((kvw, tm), lambda i: (0, i)),
                   row(IDX_HEADS * IDX_DIM), row(2 * LANES), pl.BlockSpec((LANES, tm), lambda i: (0, i))],
        out_shape=[jax.ShapeDtypeStruct((m, SA_HEADS * SA_HEAD_DIM), BF16),
                   jax.ShapeDtypeStruct((m, kvw), BF16),
                   jax.ShapeDtypeStruct((kvw, m), BF16),
                   jax.ShapeDtypeStruct((m, IDX_HEADS * IDX_DIM), BF16),
                   jax.ShapeDtypeStruct((m, 2 * LANES), BF16),
                   jax.ShapeDtypeStruct((LANES, m), F32)],
        compiler_params=_cp("parallel"),
        name="dsa_prep",
    )(proj, *tabs64, *tabs128, gq.reshape(1, LANES), gk.reshape(1, LANES), gi_pad.reshape(1, LANES))


def _dsa_attn_kernel(q_ref, k_ref, vt_ref, qi_ref, ki_ref, wt_ref, o_ref, key_ref, m_ref, l_ref, acc_ref,
                     *, tq, ksel):
    qb = pl.program_id(1)
    nkv = qb + 1
    row = lax.broadcasted_iota(I32, (tq, tq), 0)
    col = lax.broadcasted_iota(I32, (tq, tq), 1)

    def score_block(j, carry):
        start = pl.multiple_of(j * tq, tq)
        ki_lo = ki_ref[pl.ds(start, tq), :LANES]
        ki_hi = ki_ref[pl.ds(start, tq), LANES:]
        sc = jnp.zeros((tq, tq), F32)
        for hp in range(IDX_HEADS // 2):
            qp = qi_ref[:, hp * LANES:(hp + 1) * LANES]
            w0 = wt_ref[IDX_DIM + 2 * hp:IDX_DIM + 2 * hp + 1, :]
            w1 = wt_ref[IDX_DIM + 2 * hp + 1:IDX_DIM + 2 * hp + 2, :]
            sc = sc + jnp.maximum(_nt(ki_lo, qp), 0.0) * w0 + jnp.maximum(_nt(ki_hi, qp), 0.0) * w1
        bits = pltpu.bitcast(sc, I32)
        key = bits ^ ((bits >> 31) & jnp.int32(0x7FFFFFFF))
        key = jnp.where((j < qb) | (row <= col), key, jnp.int32(INT_MIN))
        key_ref[pl.ds(start, tq), :] = key
        return carry

    lax.fori_loop(0, nkv, score_block, 0)

    def bisect(it, thr):
        cand = thr + jnp.left_shift(jnp.int32(1), 31 - it)

        def count_block(j, cnt):
            start = pl.multiple_of(j * tq, tq)
            ge = jnp.where(key_ref[pl.ds(start, tq), :] >= cand, 1, 0)
            return cnt + jnp.sum(ge.reshape(tq // 8, 8, tq), axis=0)

        cnt = lax.fori_loop(0, nkv, count_block, jnp.zeros((8, tq), I32))
        total = jnp.sum(cnt, axis=0, keepdims=True)
        return jnp.where(total >= ksel, cand, thr)

    thr = lax.fori_loop(0, 32, bisect, jnp.full((1, tq), INT_MIN, I32))
    thr = jnp.maximum(thr, jnp.int32(INT_MIN + 1))

    m_ref[...] = jnp.full(m_ref.shape, NEG_BIG, F32)
    l_ref[...] = jnp.zeros(l_ref.shape, F32)
    acc_ref[...] = jnp.zeros(acc_ref.shape, F32)

    def attend_block(j, carry):
        start = pl.multiple_of(j * tq, tq)
        bias = jnp.where(key_ref[pl.ds(start, tq), :] >= thr, 0.0, NEG_BIG)
        for g in range(SA_KV_HEADS):
            kb = k_ref[pl.ds(start, tq), g * LANES:(g + 1) * LANES]
            vtb = vt_ref[g * LANES:(g + 1) * LANES, pl.ds(start, tq)]
            for r in range(SA_REP):
                h = g * SA_REP + r
                s = _nt(kb, q_ref[:, h * LANES:(h + 1) * LANES]) + bias
                m_old = m_ref[h:h + 1, :]
                m_new = jnp.maximum(m_old, jnp.max(s, axis=0, keepdims=True))
                alpha = jnp.exp(m_old - m_new)
                p = jnp.exp(s - m_new)
                l_ref[h:h + 1, :] = alpha * l_ref[h:h + 1, :] + jnp.sum(p, axis=0, keepdims=True)
                acc_ref[h] = alpha * acc_ref[h] + _dot(vtb, p.astype(BF16))
                m_ref[h:h + 1, :] = m_new
        return carry

    lax.fori_loop(0, nkv, attend_block, 0)

    for h in range(SA_HEADS):
        o = acc_ref[h] * (1.0 / l_ref[h:h + 1, :])
        o_ref[:, h * LANES:(h + 1) * LANES] = o.T.astype(BF16)


def _dsa_attn(q, k, vt, qi, ki2, wt, bsz, seq, tq=256):
    tq = min(tq, seq)
    nq = seq // tq
    ksel = min(TOPK, seq // 4)
    kvw = SA_KV_HEADS * SA_HEAD_DIM
    qrow = lambda n: pl.BlockSpec((tq, n), lambda b, i: (b * nq + i, 0))
    return pl.pallas_call(
        functools.partial(_dsa_attn_kernel, tq=tq, ksel=ksel),
        grid=(bsz, nq),
        in_specs=[qrow(SA_HEADS * SA_HEAD_DIM),
                  pl.BlockSpec((seq, kvw), lambda b, i: (b, 0)),
                  pl.BlockSpec((kvw, seq), lambda b, i: (0, b)),
                  qrow(IDX_HEADS * IDX_DIM),
                  pl.BlockSpec((seq, 2 * LANES), lambda b, i: (b, 0)),
                  pl.BlockSpec((LANES, tq), lambda b, i: (0, b * nq + i))],
        out_specs=qrow(SA_HEADS * SA_HEAD_DIM),
        out_shape=jax.ShapeDtypeStruct((bsz * seq, SA_HEADS * SA_HEAD_DIM), BF16),
        scratch_shapes=[pltpu.VMEM((seq, tq), I32), pltpu.VMEM((SA_HEADS, tq), F32),
                        pltpu.VMEM((SA_HEADS, tq), F32), pltpu.VMEM((SA_HEADS, SA_HEAD_DIM, tq), F32)],
        compiler_params=_cp("parallel", "arbitrary"),
        name="dsa_attn",
    )(q, k, vt, qi, ki2, wt)


def _diff_attention_layer(x, mod, g1, seq, bsz, tabs64, w_in, w_out, gq, gk, lam_params, subln_g, lam_init):
    shift1, scale1, gate1 = mod
    proj = _norm_proj(x, g1, scale1, shift1, w_in.astype(BF16), seq, tn=512)
    qkv = _da_prep(proj, tabs64, gq, gk)
    heads = _da_attn(qkv, lam_params, subln_g, bsz, seq, lam_init)
    return _res_matmul(heads, w_out.astype(BF16), x, gate1, seq)


def _ssd_layer(x, mod, g1, seq, bsz, w_in, conv_w, conv_b, dt_bias, a_log, d_skip, norm_g, w_out):
    shift1, scale1, gate1 = mod
    w_dt = w_in[:, SSD_DT_OFF:].reshape(D_MODEL, SSD_GROUPS, SSD_HPG)
    w_dt = jnp.pad(w_dt, ((0, 0), (0, 0), (0, LANES - SSD_HPG))).reshape(D_MODEL, SSD_GROUPS * LANES)
    w_all = jnp.concatenate([w_in[:, :SSD_DT_OFF], w_dt], axis=1).astype(BF16)
    proj = _norm_proj(x, g1, scale1, shift1, w_all, seq, tn=1024)
    xbc = _ssd_conv(proj, conv_w, conv_b, seq)
    y = _ssd_scan(xbc, proj, dt_bias, a_log, d_skip, norm_g, bsz, seq)
    return _res_matmul(y, w_out.astype(BF16), x, gate1, seq)


def _dsa_layer(x, mod, g1, seq, bsz, tabs64, tabs128, w_in, w_out, gq, gk, gi):
    shift1, scale1, gate1 = mod
    w_pad = jnp.pad(w_in, ((0, 0), (0, SA_PROJ - SA_IN))).astype(BF16)
    proj = _norm_proj(x, g1, scale1, shift1, w_pad, seq, tn=384)
    q, k, vt, qi, ki2, wt = _dsa_prep(proj, tabs64, tabs128, gq, gk, gi)
    heads = _dsa_attn(q, k, vt, qi, ki2, wt, bsz, seq)
    return _res_matmul(heads, w_out.astype(BF16), x, gate1, seq)


def kernel(x, c, positions, norm1_g, norm2_g, ada_w, ada_b, ffn_w_gate_up, ffn_w_down, da_w_in, da_w_out, da_q_norm_g, da_k_norm_g, da_lambda_q1, da_lambda_k1, da_lambda_q2, da_lambda_k2, da_subln_g, ssd_w_in, ssd_conv_w, ssd_conv_b, ssd_dt_bias, ssd_a_log, ssd_d_skip, ssd_norm_g, ssd_w_out, sa_w_in, sa_w_out, sa_q_norm_g, sa_k_norm_g, sa_idx_k_norm_g):
    bsz, seq, d = x.shape
    depth = norm1_g.shape[0]
    m = bsz * seq
    xf = x.reshape(m, d)
    pos_col = positions.reshape(m, 1).astype(I32)
    tabs64 = _rope_tables(pos_col, 64)
    tabs128 = _rope_tables(pos_col, 128)
    ada = _ada(c, ada_w, ada_b).reshape(depth, bsz, ADA_CHUNKS, 1, d)

    for i in range(depth):
        kind, j = i % N_MIXERS, i // N_MIXERS
        mod1 = tuple(ada[i, :, n] for n in range(3))
        shift2, scale2, gate2 = (ada[i, :, n] for n in range(3, 6))
        if kind == 0:
            lam_init = 0.8 - 0.6 * math.exp(-0.3 * i)
            lam_params = jnp.stack([da_lambda_q1[j], da_lambda_k1[j], da_lambda_q2[j], da_lambda_k2[j]])
            xf = _diff_attention_layer(xf, mod1, norm1_g[i], seq, bsz, tabs64, da_w_in[j], da_w_out[j],
                                       da_q_norm_g[j], da_k_norm_g[j], lam_params, da_subln_g[j], lam_init)
        elif kind == 1:
            xf = _ssd_layer(xf, mod1, norm1_g[i], seq, bsz, ssd_w_in[j], ssd_conv_w[j], ssd_conv_b[j],
                            ssd_dt_bias[j], ssd_a_log[j], ssd_d_skip[j], ssd_norm_g[j], ssd_w_out[j])
        else:
            xf = _dsa_layer(xf, mod1, norm1_g[i], seq, bsz, tabs64, tabs128, sa_w_in[j], sa_w_out[j],
                            sa_q_norm_g[j], sa_k_norm_g[j], sa_idx_k_norm_g[j])
        hid = _norm_ffn_up(xf, norm2_g[i], scale2, shift2, ffn_w_gate_up[i].astype(BF16), seq)
        xf = _res_matmul(hid, ffn_w_down[i].astype(BF16), xf, gate2, seq)
    return xf.reshape(bsz, seq, d)
```

```python
import functools
import math

import jax
import jax.numpy as jnp
from jax import lax
from jax.experimental import pallas as pl
from jax.experimental.pallas import tpu as pltpu

F32 = jnp.float32
BF16 = jnp.bfloat16
I32 = jnp.int32

D_MODEL = 2048
N_MIXERS = 3
ROPE_THETA = 500000.0
NORM_EPS = 1e-6
ADA_CHUNKS = 6
FFN_HIDDEN = 5632
LANES = 128
NEG_BIG = -1e30
INT_MIN = -2 ** 31
LOG2E = 1.4426950408889634

DA_HEADS = 16
DA_QK_DIM = 64
DA_Q_COLS = 2048
DA_IN = 6144

SSD_INNER = 4096
SSD_HEAD_DIM = 64
SSD_HEADS = 64
SSD_GROUPS = 8
SSD_HPG = 8
SSD_STATE = 128
SSD_CONV = 4
SSD_CHUNK = 128
SSD_CONV_DIM = 6144
SSD_GW = SSD_HPG * SSD_HEAD_DIM
SSD_DT_OFF = SSD_INNER + SSD_CONV_DIM
SSD_PROJ = SSD_DT_OFF + SSD_GROUPS * LANES

SA_HEADS = 16
SA_KV_HEADS = 4
SA_HEAD_DIM = 128
SA_REP = 4
IDX_HEADS = 16
IDX_DIM = 64
TOPK = 256
SA_IN = 4176
SA_PROJ = 4224
SA_QI_OFF = 3072
SA_KI_OFF = 4096

VMEM_LIMIT = 56 * 1024 * 1024


def _cp(*sem):
    return pltpu.CompilerParams(dimension_semantics=sem, vmem_limit_bytes=VMEM_LIMIT)


def _nt(a, b):
    return lax.dot_general(a, b, (((1,), (1,)), ((), ())), preferred_element_type=F32)


def _dot(a, b):
    return jnp.dot(a, b, preferred_element_type=F32)


def _silu(x):
    return x * (1.0 / (1.0 + jnp.exp(-x)))


def _rope_table_kernel(pos_ref, pat_ref, c_ref, s1_ref, s2_ref):
    ang = pos_ref[...].astype(F32) * pat_ref[0:1, :]
    s = jnp.sin(ang)
    c_ref[...] = jnp.cos(ang)
    s1_ref[...] = s * pat_ref[1:2, :]
    s2_ref[...] = s * pat_ref[2:3, :]


def _rope_pattern(head_dim):
    rot = head_dim // 4
    half = rot // 2
    lane = jnp.arange(LANES) % head_dim
    inv_freq = jnp.power(jnp.float32(ROPE_THETA), -(lane % half).astype(F32) / half)
    freq = jnp.where(lane < rot, inv_freq, 0.0)
    m1 = jnp.where(lane < half, -1.0, 0.0)
    m2 = jnp.where((lane >= half) & (lane < rot), 1.0, 0.0)
    pat = jnp.zeros((8, LANES), F32)
    return pat.at[0].set(freq).at[1].set(m1).at[2].set(m2)


def _rope_tables(pos_col, head_dim):
    m = pos_col.shape[0]
    tm = min(m, 1024)
    out = jax.ShapeDtypeStruct((m, LANES), F32)
    return pl.pallas_call(
        _rope_table_kernel,
        grid=(m // tm,),
        in_specs=[pl.BlockSpec((tm, 1), lambda i: (i, 0)), pl.BlockSpec((8, LANES), lambda i: (0, 0))],
        out_specs=[pl.BlockSpec((tm, LANES), lambda i: (i, 0))] * 3,
        out_shape=[out] * 3,
        compiler_params=_cp("parallel"),
        name="rope_tables",
    )(pos_col, _rope_pattern(head_dim))


def _rope(x, c, s1, s2, half):
    return x * c + pltpu.roll(x, LANES - half, 1) * s1 + pltpu.roll(x, half, 1) * s2


def _ada_kernel(c_ref, w_ref, b_ref, o_ref):
    c = c_ref[...]
    ca = _silu(c).astype(BF16)
    o_ref[...] = _dot(ca, w_ref[...].astype(BF16)) + b_ref[...]


def _ada(c, ada_w, ada_b):
    depth, d, n = ada_w.shape
    bsz = c.shape[0]
    tn = 1024
    return pl.pallas_call(
        _ada_kernel,
        grid=(depth, n // tn),
        in_specs=[pl.BlockSpec((bsz, d), lambda l, j: (0, 0)),
                  pl.BlockSpec((None, d, tn), lambda l, j: (l, 0, j)),
                  pl.BlockSpec((None, 1, tn), lambda l, j: (l, 0, j))],
        out_specs=pl.BlockSpec((None, bsz, tn), lambda l, j: (l, 0, j)),
        out_shape=jax.ShapeDtypeStruct((depth, bsz, n), F32),
        compiler_params=_cp("parallel", "parallel"),
        name="ada",
    )(c, ada_w, ada_b.reshape(depth, 1, n))


def _modulated_norm(x, g, sc, sh):
    ms = jnp.mean(x * x, axis=-1, keepdims=True)
    return (x * lax.rsqrt(ms + NORM_EPS) * g) * (1.0 + sc) + sh


def _norm_proj_kernel(x_ref, g_ref, sc_ref, sh_ref, w_ref, o_ref, h_ref):
    @pl.when(pl.program_id(1) == 0)
    def _():
        h_ref[...] = _modulated_norm(x_ref[...], g_ref[...], sc_ref[...], sh_ref[...]).astype(BF16)

    o_ref[...] = _dot(h_ref[...], w_ref[...]).astype(o_ref.dtype)


def _mod_specs(tm, seq, d):
    row = lambda i, j: (i, 0)
    per_batch = lambda i, j: ((i * tm) // seq, 0, 0)
    return [pl.BlockSpec((tm, d), row),
            pl.BlockSpec((1, d), lambda i, j: (0, 0)),
            pl.BlockSpec((None, 1, d), per_batch),
            pl.BlockSpec((None, 1, d), per_batch)]


def _norm_proj(x, g, scale, shift, w, seq, tn, out_dtype=F32, tm=512):
    m, d = x.shape
    n = w.shape[1]
    tm = min(tm, seq)
    return pl.pallas_call(
        _norm_proj_kernel,
        grid=(m // tm, n // tn),
        in_specs=_mod_specs(tm, seq, d) + [pl.BlockSpec((d, tn), lambda i, j: (0, j))],
        out_specs=pl.BlockSpec((tm, tn), lambda i, j: (i, j)),
        out_shape=jax.ShapeDtypeStruct((m, n), out_dtype),
        scratch_shapes=[pltpu.VMEM((tm, d), BF16)],
        compiler_params=_cp("parallel", "arbitrary"),
        name="norm_proj",
    )(x, g.reshape(1, d), scale, shift, w)


def _norm_ffn_up_kernel(x_ref, g_ref, sc_ref, sh_ref, wg_ref, wu_ref, o_ref, h_ref):
    @pl.when(pl.program_id(1) == 0)
    def _():
        h_ref[...] = _modulated_norm(x_ref[...], g_ref[...], sc_ref[...], sh_ref[...]).astype(BF16)

    h = h_ref[...]
    gate = _dot(h, wg_ref[...])
    up = _dot(h, wu_ref[...])
    o_ref[...] = (_silu(gate) * up).astype(o_ref.dtype)


def _norm_ffn_up(x, g, scale, shift, w_gate_up, seq, tn=512, tm=512):
    m, d = x.shape
    hid = w_gate_up.shape[1] // 2
    nj = hid // tn
    tm = min(tm, seq)
    return pl.pallas_call(
        _norm_ffn_up_kernel,
        grid=(m // tm, nj),
        in_specs=_mod_specs(tm, seq, d) + [pl.BlockSpec((d, tn), lambda i, j: (0, j)),
                                           pl.BlockSpec((d, tn), lambda i, j: (0, j + nj))],
        out_specs=pl.BlockSpec((tm, tn), lambda i, j: (i, j)),
        out_shape=jax.ShapeDtypeStruct((m, hid), BF16),
        scratch_shapes=[pltpu.VMEM((tm, d), BF16)],
        compiler_params=_cp("parallel", "arbitrary"),
        name="norm_ffn_up",
    )(x, g.reshape(1, d), scale, shift, w_gate_up, w_gate_up)


def _res_matmul_kernel(a_ref, w_ref, x_ref, gate_ref, o_ref):
    o_ref[...] = x_ref[...] + gate_ref[...] * _dot(a_ref[...], w_ref[...])


def _res_matmul(a, w, x, gate, seq, tn=512, tm=512):
    m, k = a.shape
    d = w.shape[1]
    tm = min(tm, seq)
    return pl.pallas_call(
        _res_matmul_kernel,
        grid=(m // tm, d // tn),
        in_specs=[pl.BlockSpec((tm, k), lambda i, j: (i, 0)),
                  pl.BlockSpec((k, tn), lambda i, j: (0, j)),
                  pl.BlockSpec((tm, tn), lambda i, j: (i, j)),
                  pl.BlockSpec((None, 1, tn), lambda i, j: ((i * tm) // seq, 0, j))],
        out_specs=pl.BlockSpec((tm, tn), lambda i, j: (i, j)),
        out_shape=jax.ShapeDtypeStruct((m, d), F32),
        compiler_params=_cp("parallel", "parallel"),
        name="res_matmul",
    )(a, w, x, gate)


def _da_prep_kernel(p_ref, c_ref, s1_ref, s2_ref, gq_ref, gk_ref, o_ref, vt_ref):
    c, s1, s2 = c_ref[...], s1_ref[...], s2_ref[...]
    tm = p_ref.shape[0]
    lo = lax.broadcasted_iota(I32, (tm, LANES), 1) < DA_QK_DIM
    n_qk = DA_Q_COLS // LANES
    for blk in range(2 * n_qk):
        x = p_ref[:, blk * LANES:(blk + 1) * LANES]
        x2 = x * x
        s_lo = jnp.sum(jnp.where(lo, x2, 0.0), axis=-1, keepdims=True)
        s_all = jnp.sum(x2, axis=-1, keepdims=True)
        ms = jnp.where(lo, s_lo, s_all - s_lo) * (1.0 / DA_QK_DIM)
        gain = gq_ref[...] if blk < n_qk else gk_ref[...]
        y = _rope(x * lax.rsqrt(ms + NORM_EPS) * gain, c, s1, s2, DA_QK_DIM // 8)
        if blk < n_qk:
            y = y * (DA_QK_DIM ** -0.5 * LOG2E)
        o_ref[:, blk * LANES:(blk + 1) * LANES] = y.astype(BF16)
    for h in range(DA_HEADS):
        v = p_ref[:, 2 * DA_Q_COLS + h * LANES:2 * DA_Q_COLS + (h + 1) * LANES]
        vt_ref[h * LANES:(h + 1) * LANES, :] = v.T.astype(BF16)


def _da_prep(proj, tabs, gq, gk, tm=256):
    m, n = proj.shape
    tm = min(tm, m)
    tab_spec = pl.BlockSpec((tm, LANES), lambda i: (i, 0))
    vec_spec = pl.BlockSpec((1, LANES), lambda i: (0, 0))
    vw = DA_HEADS * LANES
    return pl.pallas_call(
        _da_prep_kernel,
        grid=(m // tm,),
        in_specs=[pl.BlockSpec((tm, n), lambda i: (i, 0)), tab_spec, tab_spec, tab_spec, vec_spec, vec_spec],
        out_specs=[pl.BlockSpec((tm, 2 * DA_Q_COLS), lambda i: (i, 0)), pl.BlockSpec((vw, tm), lambda i: (0, i))],
        out_shape=[jax.ShapeDtypeStruct((m, 2 * DA_Q_COLS), BF16), jax.ShapeDtypeStruct((vw, m), BF16)],
        compiler_params=_cp("parallel"),
        name="da_prep",
    )(proj, *tabs, jnp.tile(gq, 2).reshape(1, LANES), jnp.tile(gk, 2).reshape(1, LANES))


def _da_attn_kernel(q_ref, k_ref, vt_ref, lam_ref, g_ref, o_ref, m_ref, l_ref, acc_ref, *, tq, tk, lam_init):
    qb = pl.program_id(2)
    q = q_ref[...]
    lane = lax.broadcasted_iota(I32, (tq, LANES), 1)
    zero = jnp.zeros_like(q)
    qs = jnp.concatenate([jnp.where(lane < DA_QK_DIM, q, zero), jnp.where(lane >= DA_QK_DIM, q, zero)], axis=0)
    m_ref[...] = jnp.full(m_ref.shape, NEG_BIG, F32)
    l_ref[...] = jnp.zeros(l_ref.shape, F32)
    acc_ref[...] = jnp.zeros(acc_ref.shape, F32)

    def step(j, masked):
        start = pl.multiple_of(j * tk, tk)
        kb = k_ref[pl.ds(start, tk), :]
        vtb = vt_ref[:, pl.ds(start, tk)]
        s = _nt(kb, qs)
        if masked:
            key = lax.broadcasted_iota(I32, (tk, 2 * tq), 0) + (j * tk - qb * tq)
            qry = lax.broadcasted_iota(I32, (tk, 2 * tq), 1)
            qry = jnp.where(qry >= tq, qry - tq, qry)
            s = jnp.where(key <= qry, s, NEG_BIG)
        m_old = m_ref[...]
        m_new = jnp.maximum(m_old, jnp.max(s, axis=0, keepdims=True))
        alpha = jnp.exp2(m_old - m_new)
        p = jnp.exp2(s - m_new)
        l_ref[...] = alpha * l_ref[...] + jnp.sum(p, axis=0, keepdims=True)
        acc_ref[...] = alpha * acc_ref[...] + _dot(vtb, p.astype(BF16))
        m_ref[...] = m_new

    def body(j, carry):
        step(j, False)
        return carry

    n_full = (qb * tq) // tk
    lax.fori_loop(0, n_full, body, 0)
    step(n_full, True)

    lam_p = lam_ref[...]
    lam = (jnp.exp(jnp.sum(lam_p[0:1] * lam_p[1:2], axis=-1, keepdims=True))
           - jnp.exp(jnp.sum(lam_p[2:3] * lam_p[3:4], axis=-1, keepdims=True)) + lam_init)
    o_all = acc_ref[...] * (1.0 / l_ref[...])
    o = o_all[:, :tq] - lam * o_all[:, tq:]
    ms = jnp.mean(o * o, axis=0, keepdims=True)
    o = o * lax.rsqrt(ms + NORM_EPS) * (g_ref[...] * (1.0 - lam_init))
    o_ref[...] = o.T.astype(BF16)


def _da_attn(qk, vt, lam_params, subln_g, bsz, seq, lam_init, tq=256, tk=512):
    tq = min(tq, seq)
    tk = min(tk, seq)
    nq = seq // tq
    nh = DA_HEADS
    return pl.pallas_call(
        functools.partial(_da_attn_kernel, tq=tq, tk=tk, lam_init=lam_init),
        grid=(bsz, nh, nq),
        in_specs=[pl.BlockSpec((tq, LANES), lambda b, h, i: (b * nq + i, h)),
                  pl.BlockSpec((seq, LANES), lambda b, h, i: (b, nh + h)),
                  pl.BlockSpec((LANES, seq), lambda b, h, i: (h, b)),
                  pl.BlockSpec((4, DA_QK_DIM), lambda b, h, i: (0, 0)),
                  pl.BlockSpec((LANES, 1), lambda b, h, i: (0, 0))],
        out_specs=pl.BlockSpec((tq, LANES), lambda b, h, i: (b * nq + i, h)),
        out_shape=jax.ShapeDtypeStruct((bsz * seq, nh * LANES), BF16),
        scratch_shapes=[pltpu.VMEM((1, 2 * tq), F32), pltpu.VMEM((1, 2 * tq), F32),
                        pltpu.VMEM((LANES, 2 * tq), F32)],
        compiler_params=_cp("parallel", "parallel", "arbitrary"),
        name="da_attn",
    )(qk, qk, vt, lam_params, subln_g.reshape(LANES, 1))


def _ssd_conv_kernel(x_ref, halo_ref, w_ref, b_ref, o_ref, xe_ref, *, ts, tiles_per_seq):
    first = (pl.program_id(0) % tiles_per_seq) == 0
    xe_ref[0:8, :] = jnp.where(first, 0.0, halo_ref[...])
    xe_ref[8:, :] = x_ref[...]
    y = b_ref[...] + w_ref[3:4, :] * x_ref[...]
    for tap in range(SSD_CONV - 1):
        y = y + w_ref[tap:tap + 1, :] * xe_ref[pl.ds(5 + tap, ts), :]
    o_ref[...] = _silu(y).astype(BF16)


def _ssd_conv(proj, conv_w, conv_b, seq, ts=512, tc=512):
    m = proj.shape[0]
    ts = min(ts, seq)
    col0 = SSD_INNER // tc
    rb = ts // 8
    return pl.pallas_call(
        functools.partial(_ssd_conv_kernel, ts=ts, tiles_per_seq=seq // ts),
        grid=(m // ts, SSD_CONV_DIM // tc),
        in_specs=[pl.BlockSpec((ts, tc), lambda i, j: (i, col0 + j)),
                  pl.BlockSpec((8, tc), lambda i, j: (jnp.maximum(i * rb - 1, 0), col0 + j)),
                  pl.BlockSpec((SSD_CONV, tc), lambda i, j: (0, j)),
                  pl.BlockSpec((1, tc), lambda i, j: (0, j))],
        out_specs=pl.BlockSpec((ts, tc), lambda i, j: (i, j)),
        out_shape=jax.ShapeDtypeStruct((m, SSD_CONV_DIM), BF16),
        scratch_shapes=[pltpu.VMEM((ts + 8, tc), F32)],
        compiler_params=_cp("parallel", "parallel"),
        name="ssd_conv",
    )(proj, proj, conv_w, conv_b.reshape(1, SSD_CONV_DIM))


def _split_bf16(x, parts):
    out = []
    for _ in range(parts):
        hi = x.astype(BF16)
        out.append(hi)
        x = x - hi.astype(F32)
    return out


def _ssd_scan_kernel(xs_ref, b_ref, c_ref, dt_ref, z_ref, bias_ref, alog_ref, dskip_ref, ng_ref, e_ref,
                     o_ref, state_ref):
    q = SSD_CHUNK

    @pl.when(pl.program_id(2) == 0)
    def _():
        state_ref[...] = jnp.zeros(state_ref.shape, F32)

    row = lax.broadcasted_iota(I32, (q, q), 0)
    col = lax.broadcasted_iota(I32, (q, q), 1)
    causal = col <= row
    lane_lo = col < SSD_HEAD_DIM

    x = dt_ref[...] + bias_ref[...]
    dt = jnp.maximum(x, 0.0) + jnp.log1p(jnp.exp(-jnp.abs(x)))
    dta = dt * (-jnp.exp(alog_ref[...]))
    tri = jnp.where(causal, 1.0, 0.0).astype(BF16)
    parts = _dot(tri, jnp.concatenate(_split_bf16(dta, 3), axis=1))
    acum = parts[:, :q] + parts[:, q:2 * q] + parts[:, 2 * q:]
    a_last = acum[q - 1:q, :]
    e_acum = jnp.exp(acum)
    w_end = dt * jnp.exp(a_last - acum)
    decay = jnp.broadcast_to(jnp.exp(a_last), (8, q))
    stacked = jnp.concatenate([e_acum, w_end, decay], axis=0)
    expanded = _dot(jnp.concatenate(_split_bf16(stacked, 2), axis=1), e_ref[...])
    e_acum_x = expanded[:q]
    w_end_x = expanded[q:2 * q]
    decay_x = expanded[2 * q:2 * q + 1]

    acum_t = acum.T
    dt_t = dt.T
    bm = b_ref[...]
    cm = c_ref[...]
    cb = _nt(cm, bm)
    xs = xs_ref[...]
    xs32 = xs.astype(F32)
    state = state_ref[...]
    y_inter = _dot(cm, state.astype(BF16)) * e_acum_x

    y_pairs = []
    for hp in range(SSD_HPG // 2):
        xs_pair = xs[:, hp * LANES:(hp + 1) * LANES]
        ys = []
        for e in range(2):
            h = 2 * hp + e
            seg = acum[:, h:h + 1] - acum_t[h:h + 1, :]
            decay_ts = jnp.exp(jnp.where(causal, seg, NEG_BIG))
            mh = (cb * decay_ts * dt_t[h:h + 1, :]).astype(BF16)
            ys.append(_dot(mh, xs_pair))
        y_pairs.append(jnp.where(lane_lo, ys[0], ys[1]))
    y = jnp.concatenate(y_pairs, axis=1) + y_inter + dskip_ref[...] * xs32

    gated = y * _silu(z_ref[...])
    ms = jnp.mean(gated * gated, axis=-1, keepdims=True)
    o_ref[...] = (gated * lax.rsqrt(ms + NORM_EPS) * ng_ref[...]).astype(BF16)

    bm_t = bm.astype(F32).T.astype(BF16)
    state_ref[...] = state * decay_x + _dot(bm_t, (xs32 * w_end_x).astype(BF16))


def _ssd_scan(xbc, proj, dt_bias, a_log, d_skip, norm_g, bsz, seq):
    q = SSD_CHUNK
    nc = seq // q
    g_n = SSD_GROUPS
    gw = SSD_GW
    pad = lambda v: jnp.pad(v.reshape(g_n, 1, SSD_HPG), ((0, 0), (0, 0), (0, LANES - SSD_HPG)))
    expand = (jnp.arange(gw)[None, :] // SSD_HEAD_DIM == jnp.arange(LANES)[:, None]).astype(BF16)
    expand2 = jnp.concatenate([expand, expand], axis=0)
    rows = lambda b, g, c: b * nc + c
    grp = lambda b, g, c: (g, 0, 0)
    return pl.pallas_call(
        _ssd_scan_kernel,
        grid=(bsz, g_n, nc),
        in_specs=[pl.BlockSpec((q, gw), lambda b, g, c: (rows(b, g, c), g)),
                  pl.BlockSpec((q, LANES), lambda b, g, c: (rows(b, g, c), SSD_INNER // LANES + g)),
                  pl.BlockSpec((q, LANES), lambda b, g, c: (rows(b, g, c), SSD_INNER // LANES + g_n + g)),
                  pl.BlockSpec((q, LANES), lambda b, g, c: (rows(b, g, c), SSD_DT_OFF // LANES + g)),
                  pl.BlockSpec((q, gw), lambda b, g, c: (rows(b, g, c), g)),
                  pl.BlockSpec((None, 1, LANES), grp),
                  pl.BlockSpec((None, 1, LANES), grp),
                  pl.BlockSpec((None, 1, gw), grp),
                  pl.BlockSpec((None, 1, gw), grp),
                  pl.BlockSpec((2 * LANES, gw), lambda b, g, c: (0, 0))],
        out_specs=pl.BlockSpec((q, gw), lambda b, g, c: (rows(b, g, c), g)),
        out_shape=jax.ShapeDtypeStruct((bsz * seq, SSD_INNER), BF16),
        scratch_shapes=[pltpu.VMEM((SSD_STATE, gw), F32)],
        compiler_params=_cp("parallel", "parallel", "arbitrary"),
        name="ssd_scan",
    )(xbc, xbc, xbc, proj, proj, pad(dt_bias), pad(a_log),
      jnp.repeat(d_skip, SSD_HEAD_DIM).reshape(g_n, 1, gw), norm_g.reshape(g_n, 1, gw), expand2)


def _dsa_prep_kernel(p_ref, c64_ref, s164_ref, s264_ref, c128_ref, s1128_ref, s2128_ref, gq_ref, gk_ref, gi_ref,
                     q_ref, k_ref, vt_ref, qi_ref, ki_ref, wt_ref):
    tm = p_ref.shape[0]
    c64, s164, s264 = c64_ref[...], s164_ref[...], s264_ref[...]
    c128, s1128, s2128 = c128_ref[...], s1128_ref[...], s2128_ref[...]

    def normed(x, gain):
        ms = jnp.mean(x * x, axis=-1, keepdims=True)
        return _rope(x * lax.rsqrt(ms + NORM_EPS) * gain, c128, s1128, s2128, SA_HEAD_DIM // 8)

    for h in range(SA_HEADS):
        x = p_ref[:, h * LANES:(h + 1) * LANES]
        q_ref[:, h * LANES:(h + 1) * LANES] = (normed(x, gq_ref[...]) * (SA_HEAD_DIM ** -0.5 * LOG2E)).astype(BF16)
    for g in range(SA_KV_HEADS):
        x = p_ref[:, (SA_HEADS + g) * LANES:(SA_HEADS + g + 1) * LANES]
        k_ref[:, g * LANES:(g + 1) * LANES] = normed(x, gk_ref[...]).astype(BF16)
        v = p_ref[:, (SA_HEADS + SA_KV_HEADS + g) * LANES:(SA_HEADS + SA_KV_HEADS + g + 1) * LANES]
        vt_ref[g * LANES:(g + 1) * LANES, :] = v.T.astype(BF16)
    for blk in range(IDX_HEADS * IDX_DIM // LANES):
        x = p_ref[:, SA_QI_OFF + blk * LANES:SA_QI_OFF + (blk + 1) * LANES]
        qi_ref[:, blk * LANES:(blk + 1) * LANES] = (
            _rope(x, c64, s164, s264, IDX_DIM // 8) * (IDX_DIM ** -0.5)).astype(BF16)
    x = p_ref[:, SA_KI_OFF:SA_KI_OFF + LANES]
    lo = lax.broadcasted_iota(I32, (tm, LANES), 1) < IDX_DIM
    xk = jnp.where(lo, x, 0.0)
    ms = jnp.sum(xk * xk, axis=-1, keepdims=True) * (1.0 / IDX_DIM)
    ki = _rope(xk * lax.rsqrt(ms + NORM_EPS) * gi_ref[...], c64, s164, s264, IDX_DIM // 8)
    ki = jnp.where(lo, ki, 0.0)
    ki_ref[:, :LANES] = ki.astype(BF16)
    ki_ref[:, LANES:] = pltpu.roll(ki, IDX_DIM, 1).astype(BF16)
    wt_ref[...] = x.T * (IDX_HEADS ** -0.5)


def _dsa_prep(proj, tabs64, tabs128, gq, gk, gi, tm=256):
    m = proj.shape[0]
    tm = min(tm, m)
    row = lambda n: pl.BlockSpec((tm, n), lambda i: (i, 0))
    vec = pl.BlockSpec((1, LANES), lambda i: (0, 0))
    kvw = SA_KV_HEADS * SA_HEAD_DIM
    gi_pad = jnp.concatenate([gi, jnp.zeros((LANES - IDX_DIM,), F32)])
    return pl.pallas_call(
        _dsa_prep_kernel,
        grid=(m // tm,),
        in_specs=[row(SA_PROJ)] + [row(LANES)] * 6 + [vec] * 3,
        out_specs=[row(SA_HEADS * SA_HEAD_DIM), row(kvw), pl.BlockSpec((kvw, tm), lambda i: (0, i)),
                   row(IDX_HEADS * IDX_DIM), row(2 * LANES), pl.BlockSpec((LANES, tm), lambda i: (0, i))],
        out_shape=[jax.ShapeDtypeStruct((m, SA_HEADS * SA_HEAD_DIM), BF16),
                   jax.ShapeDtypeStruct((m, kvw), BF16),
                   jax.ShapeDtypeStruct((kvw, m), BF16),
                   jax.ShapeDtypeStruct((m, IDX_HEADS * IDX_DIM), BF16),
                   jax.ShapeDtypeStruct((m, 2 * LANES), BF16),
                   jax.ShapeDtypeStruct((LANES, m), F32)],
        compiler_params=_cp("parallel"),
        name="dsa_prep",
    )(proj, *tabs64, *tabs128, gq.reshape(1, LANES), gk.reshape(1, LANES), gi_pad.reshape(1, LANES))


def _dsa_attn_kernel(q_ref, k_ref, vt_ref, qi_ref, ki_ref, wt_ref, o_ref, key_ref, m_ref, l_ref, acc_ref,
                     *, tq, tk, ksel):
    qb = pl.program_id(1)
    nkv = qb + 1
    row = lax.broadcasted_iota(I32, (tq, tq), 0)
    col = lax.broadcasted_iota(I32, (tq, tq), 1)

    def score_block(j, carry):
        start = pl.multiple_of(j * tq, tq)
        ki_lo = ki_ref[pl.ds(start, tq), :LANES]
        ki_hi = ki_ref[pl.ds(start, tq), LANES:]
        sc = jnp.zeros((tq, tq), F32)
        for hp in range(IDX_HEADS // 2):
            qp = qi_ref[:, hp * LANES:(hp + 1) * LANES]
            w0 = wt_ref[IDX_DIM + 2 * hp:IDX_DIM + 2 * hp + 1, :]
            w1 = wt_ref[IDX_DIM + 2 * hp + 1:IDX_DIM + 2 * hp + 2, :]
            sc = sc + jnp.maximum(_nt(ki_lo, qp), 0.0) * w0 + jnp.maximum(_nt(ki_hi, qp), 0.0) * w1
        bits = pltpu.bitcast(sc, I32)
        key = bits ^ ((bits >> 31) & jnp.int32(0x7FFFFFFF))
        key = jnp.where((j < qb) | (row <= col), key, jnp.int32(INT_MIN))
        key_ref[pl.ds(start, tq), :] = key
        return carry

    lax.fori_loop(0, nkv, score_block, 0)

    @pl.when((nkv * tq) % tk != 0)
    def _():
        key_ref[pl.ds(pl.multiple_of(nkv * tq, tq), tq), :] = jnp.full((tq, tq), INT_MIN, I32)

    def bisect(it, thr):
        cand = thr + jnp.left_shift(jnp.int32(1), 31 - it)

        def count_block(j, cnt):
            start = pl.multiple_of(j * tq, tq)
            ge = jnp.where(key_ref[pl.ds(start, tq), :] >= cand, 1, 0)
            return cnt + jnp.sum(ge.reshape(tq // 8, 8, tq), axis=0)

        cnt = lax.fori_loop(0, nkv, count_block, jnp.zeros((8, tq), I32))
        total = jnp.sum(cnt, axis=0, keepdims=True)
        return jnp.where(total >= ksel, cand, thr)

    thr = lax.fori_loop(0, 32, bisect, jnp.full((1, tq), INT_MIN, I32))
    thr = jnp.maximum(thr, jnp.int32(INT_MIN + 1))

    m_ref[...] = jnp.full(m_ref.shape, NEG_BIG, F32)
    l_ref[...] = jnp.zeros(l_ref.shape, F32)
    acc_ref[...] = jnp.zeros(acc_ref.shape, F32)

    def attend_block(j, carry):
        start = pl.multiple_of(j * tk, tk)
        bias = jnp.where(key_ref[pl.ds(start, tk), :] >= thr, 0.0, NEG_BIG)
        for g in range(SA_KV_HEADS):
            kb = k_ref[pl.ds(start, tk), g * LANES:(g + 1) * LANES]
            vtb = vt_ref[g * LANES:(g + 1) * LANES, pl.ds(start, tk)]
            for r in range(SA_REP):
                h = g * SA_REP + r
                s = _nt(kb, q_ref[:, h * LANES:(h + 1) * LANES]) + bias
                m_old = m_ref[h:h + 1, :]
                m_new = jnp.maximum(m_old, jnp.max(s, axis=0, keepdims=True))
                alpha = jnp.exp2(m_old - m_new)
                p = jnp.exp2(s - m_new)
                l_ref[h:h + 1, :] = alpha * l_ref[h:h + 1, :] + jnp.sum(p, axis=0, keepdims=True)
                acc_ref[h] = alpha * acc_ref[h] + _dot(vtb, p.astype(BF16))
                m_ref[h:h + 1, :] = m_new
        return carry

    lax.fori_loop(0, (nkv * tq + tk - 1) // tk, attend_block, 0)

    for h in range(SA_HEADS):
        o = acc_ref[h] * (1.0 / l_ref[h:h + 1, :])
        o_ref[:, h * LANES:(h + 1) * LANES] = o.T.astype(BF16)


def _dsa_attn(q, k, vt, qi, ki2, wt, bsz, seq, tq=256, tk=512):
    tq = min(tq, seq)
    tk = min(tk, seq)
    assert tk in (tq, 2 * tq)
    nq = seq // tq
    ksel = min(TOPK, seq // 4)
    kvw = SA_KV_HEADS * SA_HEAD_DIM
    qrow = lambda n: pl.BlockSpec((tq, n), lambda b, i: (b * nq + i, 0))
    return pl.pallas_call(
        functools.partial(_dsa_attn_kernel, tq=tq, tk=tk, ksel=ksel),
        grid=(bsz, nq),
        in_specs=[qrow(SA_HEADS * SA_HEAD_DIM),
                  pl.BlockSpec((seq, kvw), lambda b, i: (b, 0)),
                  pl.BlockSpec((kvw, seq), lambda b, i: (0, b)),
                  qrow(IDX_HEADS * IDX_DIM),
                  pl.BlockSpec((seq, 2 * LANES), lambda b, i: (b, 0)),
                  pl.BlockSpec((LANES, tq), lambda b, i: (0, b * nq + i))],
        out_specs=qrow(SA_HEADS * SA_HEAD_DIM),
        out_shape=jax.ShapeDtypeStruct((bsz * seq, SA_HEADS * SA_HEAD_DIM), BF16),
        scratch_shapes=[pltpu.VMEM((seq, tq), I32), pltpu.VMEM((SA_HEADS, tq), F32),
                        pltpu.VMEM((SA_HEADS, tq), F32), pltpu.VMEM((SA_HEADS, SA_HEAD_DIM, tq), F32)],
        compiler_params=_cp("parallel", "arbitrary"),
        name="dsa_attn",
    )(q, k, vt, qi, ki2, wt)


def _diff_attention_layer(x, mod, g1, seq, bsz, tabs64, w_in, w_out, gq, gk, lam_params, subln_g, lam_init):
    shift1, scale1, gate1 = mod
    proj = _norm_proj(x, g1, scale1, shift1, w_in.astype(BF16), seq, tn=512)
    qk, vt = _da_prep(proj, tabs64, gq, gk)
    heads = _da_attn(qk, vt, lam_params, subln_g, bsz, seq, lam_init)
    return _res_matmul(heads, w_out.astype(BF16), x, gate1, seq)


def _ssd_layer(x, mod, g1, seq, bsz, w_in, conv_w, conv_b, dt_bias, a_log, d_skip, norm_g, w_out):
    shift1, scale1, gate1 = mod
    w_dt = w_in[:, SSD_DT_OFF:].reshape(D_MODEL, SSD_GROUPS, SSD_HPG)
    w_dt = jnp.pad(w_dt, ((0, 0), (0, 0), (0, LANES - SSD_HPG))).reshape(D_MODEL, SSD_GROUPS * LANES)
    w_all = jnp.concatenate([w_in[:, :SSD_DT_OFF], w_dt], axis=1).astype(BF16)
    proj = _norm_proj(x, g1, scale1, shift1, w_all, seq, tn=1024)
    xbc = _ssd_conv(proj, conv_w, conv_b, seq)
    y = _ssd_scan(xbc, proj, dt_bias, a_log, d_skip, norm_g, bsz, seq)
    return _res_matmul(y, w_out.astype(BF16), x, gate1, seq)


def _dsa_layer(x, mod, g1, seq, bsz, tabs64, tabs128, w_in, w_out, gq, gk, gi):
    shift1, scale1, gate1 = mod
    w_pad = jnp.pad(w_in, ((0, 0), (0, SA_PROJ - SA_IN))).astype(BF16)
    proj = _norm_proj(x, g1, scale1, shift1, w_pad, seq, tn=384)
    q, k, vt, qi, ki2, wt = _dsa_prep(proj, tabs64, tabs128, gq, gk, gi)
    heads = _dsa_attn(q, k, vt, qi, ki2, wt, bsz, seq)
    return _res_matmul(heads, w_out.astype(BF16), x, gate1, seq)


def kernel(x, c, positions, norm1_g, norm2_g, ada_w, ada_b, ffn_w_gate_up, ffn_w_down, da_w_in, da_w_out, da_q_norm_g, da_k_norm_g, da_lambda_q1, da_lambda_k1, da_lambda_q2, da_lambda_k2, da_subln_g, ssd_w_in, ssd_conv_w, ssd_conv_b, ssd_dt_bias, ssd_a_log, ssd_d_skip, ssd_norm_g, ssd_w_out, sa_w_in, sa_w_out, sa_q_norm_g, sa_k_norm_g, sa_idx_k_norm_g):
    bsz, seq, d = x.shape
    depth = norm1_g.shape[0]
    m = bsz * seq
    xf = x.reshape(m, d)
    pos_col = positions.reshape(m, 1).astype(I32)
    tabs64 = _rope_tables(pos_col, 64)
    tabs128 = _rope_tables(pos_col, 128)
    ada = _ada(c, ada_w, ada_b).reshape(depth, bsz, ADA_CHUNKS, 1, d)

    for i in range(depth):
        kind, j = i % N_MIXERS, i // N_MIXERS
        mod1 = tuple(ada[i, :, n] for n in range(3))
        shift2, scale2, gate2 = (ada[i, :, n] for n in range(3, 6))
        if kind == 0:
            lam_init = 0.8 - 0.6 * math.exp(-0.3 * i)
            lam_params = jnp.stack([da_lambda_q1[j], da_lambda_k1[j], da_lambda_q2[j], da_lambda_k2[j]])
            xf = _diff_attention_layer(xf, mod1, norm1_g[i], seq, bsz, tabs64, da_w_in[j], da_w_out[j],
                                       da_q_norm_g[j], da_k_norm_g[j], lam_params, da_subln_g[j], lam_init)
        elif kind == 1:
            xf = _ssd_layer(xf, mod1, norm1_g[i], seq, bsz, ssd_w_in[j], ssd_conv_w[j], ssd_conv_b[j],
                            ssd_dt_bias[j], ssd_a_log[j], ssd_d_skip[j], ssd_norm_g[j], ssd_w_out[j])
        else:
            xf = _dsa_layer(xf, mod1, norm1_g[i], seq, bsz, tabs64, tabs128, sa_w_in[j], sa_w_out[j],
                            sa_q_norm_g[j], sa_k_norm_g[j], sa_idx_k_norm_g[j])
        hid = _norm_ffn_up(xf, norm2_g[i], scale2, shift2, ffn_w_gate_up[i].astype(BF16), seq)
        xf = _res_matmul(hid, ffn_w_down[i].astype(BF16), xf, gate2, seq)
    return xf.reshape(bsz, seq, d)
```

```python
import functools
import math

import jax
import jax.numpy as jnp
from jax import lax
from jax.experimental import pallas as pl
from jax.experimental.pallas import tpu as pltpu

F32 = jnp.float32
BF16 = jnp.bfloat16
I32 = jnp.int32

D_MODEL = 2048
N_MIXERS = 3
ROPE_THETA = 500000.0
NORM_EPS = 1e-6
ADA_CHUNKS = 6
FFN_HIDDEN = 5632
LANES = 128
NEG_BIG = -1e30
INT_MIN = -2 ** 31
LOG2E = 1.4426950408889634
SUM_ROWS = 16

DA_HEADS = 16
DA_QK_DIM = 64
DA_Q_COLS = 2048
DA_IN = 6144

SSD_INNER = 4096
SSD_HEAD_DIM = 64
SSD_HEADS = 64
SSD_GROUPS = 8
SSD_HPG = 8
SSD_STATE = 128
SSD_CONV = 4
SSD_CHUNK = 128
SSD_CONV_DIM = 6144
SSD_GW = SSD_HPG * SSD_HEAD_DIM
SSD_DT_OFF = SSD_INNER + SSD_CONV_DIM
SSD_PROJ = SSD_DT_OFF + SSD_GROUPS * LANES

SA_HEADS = 16
SA_KV_HEADS = 4
SA_HEAD_DIM = 128
SA_REP = 4
IDX_HEADS = 16
IDX_DIM = 64
TOPK = 256
SA_IN = 4176
SA_PROJ = 4224
SA_QI_OFF = 3072
SA_KI_OFF = 4096

VMEM_LIMIT = 56 * 1024 * 1024


def _cp(*sem):
    return pltpu.CompilerParams(dimension_semantics=sem, vmem_limit_bytes=VMEM_LIMIT)


def _nt(a, b):
    return lax.dot_general(a, b, (((1,), (1,)), ((), ())), preferred_element_type=F32)


def _dot(a, b):
    return jnp.dot(a, b, preferred_element_type=F32)


def _silu(x):
    return x * (1.0 / (1.0 + jnp.exp(-x)))


def _rope_table_kernel(pos_ref, pat_ref, c_ref, s1_ref, s2_ref):
    ang = pos_ref[...].astype(F32) * pat_ref[0:1, :]
    s = jnp.sin(ang)
    c_ref[...] = jnp.cos(ang)
    s1_ref[...] = s * pat_ref[1:2, :]
    s2_ref[...] = s * pat_ref[2:3, :]


def _rope_pattern(head_dim):
    rot = head_dim // 4
    half = rot // 2
    lane = jnp.arange(LANES) % head_dim
    inv_freq = jnp.power(jnp.float32(ROPE_THETA), -(lane % half).astype(F32) / half)
    freq = jnp.where(lane < rot, inv_freq, 0.0)
    m1 = jnp.where(lane < half, -1.0, 0.0)
    m2 = jnp.where((lane >= half) & (lane < rot), 1.0, 0.0)
    pat = jnp.zeros((8, LANES), F32)
    return pat.at[0].set(freq).at[1].set(m1).at[2].set(m2)


def _rope_tables(pos_col, head_dim):
    m = pos_col.shape[0]
    tm = min(m, 1024)
    out = jax.ShapeDtypeStruct((m, LANES), F32)
    return pl.pallas_call(
        _rope_table_kernel,
        grid=(m // tm,),
        in_specs=[pl.BlockSpec((tm, 1), lambda i: (i, 0)), pl.BlockSpec((8, LANES), lambda i: (0, 0))],
        out_specs=[pl.BlockSpec((tm, LANES), lambda i: (i, 0))] * 3,
        out_shape=[out] * 3,
        compiler_params=_cp("parallel"),
        name="rope_tables",
    )(pos_col, _rope_pattern(head_dim))


def _rope(x, c, s1, s2, half):
    return x * c + pltpu.roll(x, LANES - half, 1) * s1 + pltpu.roll(x, half, 1) * s2


def _ada_kernel(c_ref, w_ref, b_ref, o_ref):
    c = c_ref[...]
    ca = _silu(c).astype(BF16)
    o_ref[...] = _dot(ca, w_ref[...].astype(BF16)) + b_ref[...]


def _ada(c, ada_w, ada_b):
    depth, d, n = ada_w.shape
    bsz = c.shape[0]
    tn = 1024
    return pl.pallas_call(
        _ada_kernel,
        grid=(depth, n // tn),
        in_specs=[pl.BlockSpec((bsz, d), lambda l, j: (0, 0)),
                  pl.BlockSpec((None, d, tn), lambda l, j: (l, 0, j)),
                  pl.BlockSpec((None, 1, tn), lambda l, j: (l, 0, j))],
        out_specs=pl.BlockSpec((None, bsz, tn), lambda l, j: (l, 0, j)),
        out_shape=jax.ShapeDtypeStruct((depth, bsz, n), F32),
        compiler_params=_cp("parallel", "parallel"),
        name="ada",
    )(c, ada_w, ada_b.reshape(depth, 1, n))


def _modulated_norm(x, g, sc, sh):
    ms = jnp.mean(x * x, axis=-1, keepdims=True)
    return (x * lax.rsqrt(ms + NORM_EPS) * g) * (1.0 + sc) + sh


def _norm_proj_kernel(x_ref, g_ref, sc_ref, sh_ref, w_ref, o_ref, h_ref):
    @pl.when(pl.program_id(1) == 0)
    def _():
        h_ref[...] = _modulated_norm(x_ref[...], g_ref[...], sc_ref[...], sh_ref[...]).astype(BF16)

    o_ref[...] = _dot(h_ref[...], w_ref[...]).astype(o_ref.dtype)


def _mod_specs(tm, seq, d):
    row = lambda i, j: (i, 0)
    per_batch = lambda i, j: ((i * tm) // seq, 0, 0)
    return [pl.BlockSpec((tm, d), row),
            pl.BlockSpec((1, d), lambda i, j: (0, 0)),
            pl.BlockSpec((None, 1, d), per_batch),
            pl.BlockSpec((None, 1, d), per_batch)]


def _norm_proj(x, g, scale, shift, w, seq, tn, out_dtype=F32, tm=1024):
    m, d = x.shape
    n = w.shape[1]
    tm = min(tm, seq)
    return pl.pallas_call(
        _norm_proj_kernel,
        grid=(m // tm, n // tn),
        in_specs=_mod_specs(tm, seq, d) + [pl.BlockSpec((d, tn), lambda i, j: (0, j))],
        out_specs=pl.BlockSpec((tm, tn), lambda i, j: (i, j)),
        out_shape=jax.ShapeDtypeStruct((m, n), out_dtype),
        scratch_shapes=[pltpu.VMEM((tm, d), BF16)],
        compiler_params=_cp("parallel", "arbitrary"),
        name="norm_proj",
    )(x, g.reshape(1, d), scale, shift, w)


def _norm_ffn_up_kernel(x_ref, g_ref, sc_ref, sh_ref, wg_ref, wu_ref, o_ref, h_ref):
    @pl.when(pl.program_id(1) == 0)
    def _():
        h_ref[...] = _modulated_norm(x_ref[...], g_ref[...], sc_ref[...], sh_ref[...]).astype(BF16)

    h = h_ref[...]
    gate = _dot(h, wg_ref[...])
    up = _dot(h, wu_ref[...])
    o_ref[...] = (_silu(gate) * up).astype(o_ref.dtype)


def _norm_ffn_up(x, g, scale, shift, w_gate_up, seq, tn=512, tm=1024):
    m, d = x.shape
    hid = w_gate_up.shape[1] // 2
    nj = hid // tn
    tm = min(tm, seq)
    return pl.pallas_call(
        _norm_ffn_up_kernel,
        grid=(m // tm, nj),
        in_specs=_mod_specs(tm, seq, d) + [pl.BlockSpec((d, tn), lambda i, j: (0, j)),
                                           pl.BlockSpec((d, tn), lambda i, j: (0, j + nj))],
        out_specs=pl.BlockSpec((tm, tn), lambda i, j: (i, j)),
        out_shape=jax.ShapeDtypeStruct((m, hid), BF16),
        scratch_shapes=[pltpu.VMEM((tm, d), BF16)],
        compiler_params=_cp("parallel", "arbitrary"),
        name="norm_ffn_up",
    )(x, g.reshape(1, d), scale, shift, w_gate_up, w_gate_up)


def _res_matmul_kernel(a_ref, w_ref, x_ref, gate_ref, o_ref):
    o_ref[...] = x_ref[...] + gate_ref[...] * _dot(a_ref[...], w_ref[...])


def _res_matmul(a, w, x, gate, seq, tn=None, tm=1024):
    m, k = a.shape
    d = w.shape[1]
    tm = min(tm, seq)
    if tn is None:
        tn = 1024 if k <= 4096 else 512
    return pl.pallas_call(
        _res_matmul_kernel,
        grid=(m // tm, d // tn),
        in_specs=[pl.BlockSpec((tm, k), lambda i, j: (i, 0)),
                  pl.BlockSpec((k, tn), lambda i, j: (0, j)),
                  pl.BlockSpec((tm, tn), lambda i, j: (i, j)),
                  pl.BlockSpec((None, 1, tn), lambda i, j: ((i * tm) // seq, 0, j))],
        out_specs=pl.BlockSpec((tm, tn), lambda i, j: (i, j)),
        out_shape=jax.ShapeDtypeStruct((m, d), F32),
        compiler_params=_cp("parallel", "parallel"),
        name="res_matmul",
    )(a, w, x, gate)


def _da_prep_kernel(p_ref, c_ref, s1_ref, s2_ref, gq_ref, gk_ref, o_ref, vt_ref):
    c, s1, s2 = c_ref[...], s1_ref[...], s2_ref[...]
    tm = p_ref.shape[0]
    lo = lax.broadcasted_iota(I32, (tm, LANES), 1) < DA_QK_DIM
    n_qk = DA_Q_COLS // LANES
    for blk in range(2 * n_qk):
        x = p_ref[:, blk * LANES:(blk + 1) * LANES]
        x2 = x * x
        s_lo = jnp.sum(jnp.where(lo, x2, 0.0), axis=-1, keepdims=True)
        s_all = jnp.sum(x2, axis=-1, keepdims=True)
        ms = jnp.where(lo, s_lo, s_all - s_lo) * (1.0 / DA_QK_DIM)
        gain = gq_ref[...] if blk < n_qk else gk_ref[...]
        y = _rope(x * lax.rsqrt(ms + NORM_EPS) * gain, c, s1, s2, DA_QK_DIM // 8)
        if blk < n_qk:
            y = y * (DA_QK_DIM ** -0.5 * LOG2E)
        o_ref[:, blk * LANES:(blk + 1) * LANES] = y.astype(BF16)
    for h in range(DA_HEADS):
        v = p_ref[:, 2 * DA_Q_COLS + h * LANES:2 * DA_Q_COLS + (h + 1) * LANES]
        vt_ref[h * LANES:(h + 1) * LANES, :] = v.T.astype(BF16)


def _da_prep(proj, tabs, gq, gk, tm=256):
    m, n = proj.shape
    tm = min(tm, m)
    tab_spec = pl.BlockSpec((tm, LANES), lambda i: (i, 0))
    vec_spec = pl.BlockSpec((1, LANES), lambda i: (0, 0))
    vw = DA_HEADS * LANES
    return pl.pallas_call(
        _da_prep_kernel,
        grid=(m // tm,),
        in_specs=[pl.BlockSpec((tm, n), lambda i: (i, 0)), tab_spec, tab_spec, tab_spec, vec_spec, vec_spec],
        out_specs=[pl.BlockSpec((tm, 2 * DA_Q_COLS), lambda i: (i, 0)), pl.BlockSpec((vw, tm), lambda i: (0, i))],
        out_shape=[jax.ShapeDtypeStruct((m, 2 * DA_Q_COLS), BF16), jax.ShapeDtypeStruct((vw, m), BF16)],
        compiler_params=_cp("parallel"),
        name="da_prep",
    )(proj, *tabs, jnp.tile(gq, 2).reshape(1, LANES), jnp.tile(gk, 2).reshape(1, LANES))


def _da_attn_kernel(q_ref, k_ref, vt_ref, lam_ref, g_ref, o_ref, m_ref, acc_ref, *, tq, tk, hp, lam_init):
    qb = pl.program_id(2)
    lane = lax.broadcasted_iota(I32, (tq, LANES), 1)
    qs = []
    for e in range(hp):
        q = q_ref[:, e * LANES:(e + 1) * LANES]
        zero = jnp.zeros_like(q)
        qs.append(jnp.concatenate([jnp.where(lane < DA_QK_DIM, q, zero), jnp.where(lane >= DA_QK_DIM, q, zero)], axis=0))
    m_ref[...] = jnp.full(m_ref.shape, NEG_BIG, F32)
    acc_ref[...] = jnp.zeros(acc_ref.shape, F32)

    def step(j, masked):
        start = pl.multiple_of(j * tk, tk)

        def qk(e):
            return _nt(k_ref[pl.ds(start, tk), e * LANES:(e + 1) * LANES], qs[e])

        def softmax_pv(e, s):
            vtb = vt_ref[e * LANES:(e + 1) * LANES, pl.ds(start, tk)]
            vtb = jnp.concatenate([vtb, jnp.ones((SUM_ROWS, tk), BF16)], axis=0)
            if masked:
                key = lax.broadcasted_iota(I32, (tk, 2 * tq), 0) + (j * tk - qb * tq)
                qry = lax.broadcasted_iota(I32, (tk, 2 * tq), 1)
                qry = jnp.where(qry >= tq, qry - tq, qry)
                s = jnp.where(key <= qry, s, NEG_BIG)
            m_old = m_ref[e:e + 1, :]
            m_new = jnp.maximum(m_old, jnp.max(s, axis=0, keepdims=True))
            alpha = jnp.exp2(m_old - m_new)
            p = jnp.exp2((s - m_new).astype(BF16))
            acc_ref[e] = alpha * acc_ref[e] + _dot(vtb, p)
            m_ref[e:e + 1, :] = m_new

        s_next = qk(0)
        for e in range(hp):
            s_cur = s_next
            if e + 1 < hp:
                s_next = qk(e + 1)
            softmax_pv(e, s_cur)

    def body(j, carry):
        step(j, False)
        return carry

    n_full = (qb * tq) // tk
    lax.fori_loop(0, n_full, body, 0)
    step(n_full, True)

    lam_p = lam_ref[...]
    lam = (jnp.exp(jnp.sum(lam_p[0:1] * lam_p[1:2], axis=-1, keepdims=True))
           - jnp.exp(jnp.sum(lam_p[2:3] * lam_p[3:4], axis=-1, keepdims=True)) + lam_init)
    for e in range(hp):
        acc = acc_ref[e]
        o_all = acc[:LANES] * (1.0 / acc[LANES:LANES + 1])
        o = o_all[:, :tq] - lam * o_all[:, tq:]
        ms = jnp.mean(o * o, axis=0, keepdims=True)
        o = o * lax.rsqrt(ms + NORM_EPS) * (g_ref[...] * (1.0 - lam_init))
        o_ref[:, e * LANES:(e + 1) * LANES] = o.T.astype(BF16)


def _da_attn(qk, vt, lam_params, subln_g, bsz, seq, lam_init, tq=256, tk=512, hp=4):
    tq = min(tq, seq)
    tk = min(tk, seq)
    nq = seq // tq
    nh = DA_HEADS // hp
    w = hp * LANES
    return pl.pallas_call(
        functools.partial(_da_attn_kernel, tq=tq, tk=tk, hp=hp, lam_init=lam_init),
        grid=(bsz, nh, nq),
        in_specs=[pl.BlockSpec((tq, w), lambda b, h, i: (b * nq + i, h)),
                  pl.BlockSpec((seq, w), lambda b, h, i: (b, nh + h)),
                  pl.BlockSpec((w, seq), lambda b, h, i: (h, b)),
                  pl.BlockSpec((4, DA_QK_DIM), lambda b, h, i: (0, 0)),
                  pl.BlockSpec((LANES, 1), lambda b, h, i: (0, 0))],
        out_specs=pl.BlockSpec((tq, w), lambda b, h, i: (b * nq + i, h)),
        out_shape=jax.ShapeDtypeStruct((bsz * seq, DA_HEADS * LANES), BF16),
        scratch_shapes=[pltpu.VMEM((hp, 2 * tq), F32), pltpu.VMEM((hp, LANES + SUM_ROWS, 2 * tq), F32)],
        compiler_params=_cp("parallel", "parallel", "arbitrary"),
        name="da_attn",
    )(qk, qk, vt, lam_params, subln_g.reshape(LANES, 1))


def _ssd_conv_kernel(x_ref, halo_ref, w_ref, b_ref, o_ref, xe_ref, *, ts, tiles_per_seq):
    first = (pl.program_id(0) % tiles_per_seq) == 0
    xe_ref[0:8, :] = jnp.where(first, 0.0, halo_ref[...])
    xe_ref[8:, :] = x_ref[...]
    y = b_ref[...] + w_ref[3:4, :] * x_ref[...]
    for tap in range(SSD_CONV - 1):
        y = y + w_ref[tap:tap + 1, :] * xe_ref[pl.ds(5 + tap, ts), :]
    o_ref[...] = _silu(y).astype(BF16)


def _ssd_conv(proj, conv_w, conv_b, seq, ts=512, tc=512):
    m = proj.shape[0]
    ts = min(ts, seq)
    col0 = SSD_INNER // tc
    rb = ts // 8
    return pl.pallas_call(
        functools.partial(_ssd_conv_kernel, ts=ts, tiles_per_seq=seq // ts),
        grid=(m // ts, SSD_CONV_DIM // tc),
        in_specs=[pl.BlockSpec((ts, tc), lambda i, j: (i, col0 + j)),
                  pl.BlockSpec((8, tc), lambda i, j: (jnp.maximum(i * rb - 1, 0), col0 + j)),
                  pl.BlockSpec((SSD_CONV, tc), lambda i, j: (0, j)),
                  pl.BlockSpec((1, tc), lambda i, j: (0, j))],
        out_specs=pl.BlockSpec((ts, tc), lambda i, j: (i, j)),
        out_shape=jax.ShapeDtypeStruct((m, SSD_CONV_DIM), BF16),
        scratch_shapes=[pltpu.VMEM((ts + 8, tc), F32)],
        compiler_params=_cp("parallel", "parallel"),
        name="ssd_conv",
    )(proj, proj, conv_w, conv_b.reshape(1, SSD_CONV_DIM))


def _split_bf16(x, parts):
    out = []
    for _ in range(parts):
        hi = x.astype(BF16)
        out.append(hi)
        x = x - hi.astype(F32)
    return out


def _ssd_scan_kernel(xs_ref, b_ref, c_ref, dt_ref, z_ref, bias_ref, alog_ref, dskip_ref, ng_ref, e_ref,
                     o_ref, state_ref):
    q = SSD_CHUNK

    @pl.when(pl.program_id(2) == 0)
    def _():
        state_ref[...] = jnp.zeros(state_ref.shape, F32)

    row = lax.broadcasted_iota(I32, (q, q), 0)
    col = lax.broadcasted_iota(I32, (q, q), 1)
    causal = col <= row
    lane_lo = col < SSD_HEAD_DIM

    x = dt_ref[...] + bias_ref[...]
    dt = jnp.maximum(x, 0.0) + jnp.log1p(jnp.exp(-jnp.abs(x)))
    dta = dt * (-jnp.exp(alog_ref[...]))
    tri = jnp.where(causal, 1.0, 0.0).astype(BF16)
    parts = _dot(tri, jnp.concatenate(_split_bf16(dta, 3), axis=1))
    acum = parts[:, :q] + parts[:, q:2 * q] + parts[:, 2 * q:]
    a_last = acum[q - 1:q, :]
    e_acum = jnp.exp(acum)
    w_end = dt * jnp.exp(a_last - acum)
    decay = jnp.broadcast_to(jnp.exp(a_last), (8, q))
    stacked = jnp.concatenate([e_acum, w_end, decay], axis=0)
    expanded = _dot(jnp.concatenate(_split_bf16(stacked, 2), axis=1), e_ref[...])
    e_acum_x = expanded[:q]
    w_end_x = expanded[q:2 * q]
    decay_x = expanded[2 * q:2 * q + 1]

    acum_t = acum.T
    dt_t = dt.T
    bm = b_ref[...]
    cm = c_ref[...]
    cb = _nt(cm, bm)
    xs = xs_ref[...]
    xs32 = xs.astype(F32)
    state = state_ref[...]
    y_inter = _dot(cm, state.astype(BF16)) * e_acum_x

    y_pairs = []
    for hp in range(SSD_HPG // 2):
        xs_pair = xs[:, hp * LANES:(hp + 1) * LANES]
        ys = []
        for e in range(2):
            h = 2 * hp + e
            seg = acum[:, h:h + 1] - acum_t[h:h + 1, :]
            decay_ts = jnp.exp(jnp.where(causal, seg, NEG_BIG))
            mh = (cb * decay_ts * dt_t[h:h + 1, :]).astype(BF16)
            ys.append(_dot(mh, xs_pair))
        y_pairs.append(jnp.where(lane_lo, ys[0], ys[1]))
    y = jnp.concatenate(y_pairs, axis=1) + y_inter + dskip_ref[...] * xs32

    gated = y * _silu(z_ref[...])
    ms = jnp.mean(gated * gated, axis=-1, keepdims=True)
    o_ref[...] = (gated * lax.rsqrt(ms + NORM_EPS) * ng_ref[...]).astype(BF16)

    bm_t = bm.astype(F32).T.astype(BF16)
    state_ref[...] = state * decay_x + _dot(bm_t, (xs32 * w_end_x).astype(BF16))


def _ssd_scan(xbc, proj, dt_bias, a_log, d_skip, norm_g, bsz, seq):
    q = SSD_CHUNK
    nc = seq // q
    g_n = SSD_GROUPS
    gw = SSD_GW
    pad = lambda v: jnp.pad(v.reshape(g_n, 1, SSD_HPG), ((0, 0), (0, 0), (0, LANES - SSD_HPG)))
    expand = (jnp.arange(gw)[None, :] // SSD_HEAD_DIM == jnp.arange(LANES)[:, None]).astype(BF16)
    expand2 = jnp.concatenate([expand, expand], axis=0)
    rows = lambda b, g, c: b * nc + c
    grp = lambda b, g, c: (g, 0, 0)
    return pl.pallas_call(
        _ssd_scan_kernel,
        grid=(bsz, g_n, nc),
        in_specs=[pl.BlockSpec((q, gw), lambda b, g, c: (rows(b, g, c), g)),
                  pl.BlockSpec((q, LANES), lambda b, g, c: (rows(b, g, c), SSD_INNER // LANES + g)),
                  pl.BlockSpec((q, LANES), lambda b, g, c: (rows(b, g, c), SSD_INNER // LANES + g_n + g)),
                  pl.BlockSpec((q, LANES), lambda b, g, c: (rows(b, g, c), SSD_DT_OFF // LANES + g)),
                  pl.BlockSpec((q, gw), lambda b, g, c: (rows(b, g, c), g)),
                  pl.BlockSpec((None, 1, LANES), grp),
                  pl.BlockSpec((None, 1, LANES), grp),
                  pl.BlockSpec((None, 1, gw), grp),
                  pl.BlockSpec((None, 1, gw), grp),
                  pl.BlockSpec((2 * LANES, gw), lambda b, g, c: (0, 0))],
        out_specs=pl.BlockSpec((q, gw), lambda b, g, c: (rows(b, g, c), g)),
        out_shape=jax.ShapeDtypeStruct((bsz * seq, SSD_INNER), BF16),
        scratch_shapes=[pltpu.VMEM((SSD_STATE, gw), F32)],
        compiler_params=_cp("parallel", "parallel", "arbitrary"),
        name="ssd_scan",
    )(xbc, xbc, xbc, proj, proj, pad(dt_bias), pad(a_log),
      jnp.repeat(d_skip, SSD_HEAD_DIM).reshape(g_n, 1, gw), norm_g.reshape(g_n, 1, gw), expand2)


def _dsa_prep_kernel(p_ref, c64_ref, s164_ref, s264_ref, c128_ref, s1128_ref, s2128_ref, gq_ref, gk_ref, gi_ref,
                     q_ref, k_ref, vt_ref, qi_ref, ki_ref, wt_ref):
    tm = p_ref.shape[0]
    c64, s164, s264 = c64_ref[...], s164_ref[...], s264_ref[...]
    c128, s1128, s2128 = c128_ref[...], s1128_ref[...], s2128_ref[...]

    def normed(x, gain):
        ms = jnp.mean(x * x, axis=-1, keepdims=True)
        return _rope(x * lax.rsqrt(ms + NORM_EPS) * gain, c128, s1128, s2128, SA_HEAD_DIM // 8)

    for h in range(SA_HEADS):
        x = p_ref[:, h * LANES:(h + 1) * LANES]
        q_ref[:, h * LANES:(h + 1) * LANES] = (normed(x, gq_ref[...]) * (SA_HEAD_DIM ** -0.5 * LOG2E)).astype(BF16)
    for g in range(SA_KV_HEADS):
        x = p_ref[:, (SA_HEADS + g) * LANES:(SA_HEADS + g + 1) * LANES]
        k_ref[:, g * LANES:(g + 1) * LANES] = normed(x, gk_ref[...]).astype(BF16)
        v = p_ref[:, (SA_HEADS + SA_KV_HEADS + g) * LANES:(SA_HEADS + SA_KV_HEADS + g + 1) * LANES]
        vt_ref[g * LANES:(g + 1) * LANES, :] = v.T.astype(BF16)
    for blk in range(IDX_HEADS * IDX_DIM // LANES):
        x = p_ref[:, SA_QI_OFF + blk * LANES:SA_QI_OFF + (blk + 1) * LANES]
        qi_ref[:, blk * LANES:(blk + 1) * LANES] = (
            _rope(x, c64, s164, s264, IDX_DIM // 8) * (IDX_DIM ** -0.5)).astype(BF16)
    x = p_ref[:, SA_KI_OFF:SA_KI_OFF + LANES]
    lo = lax.broadcasted_iota(I32, (tm, LANES), 1) < IDX_DIM
    xk = jnp.where(lo, x, 0.0)
    ms = jnp.sum(xk * xk, axis=-1, keepdims=True) * (1.0 / IDX_DIM)
    ki = _rope(xk * lax.rsqrt(ms + NORM_EPS) * gi_ref[...], c64, s164, s264, IDX_DIM // 8)
    ki = jnp.where(lo, ki, 0.0)
    ki_ref[:, :LANES] = ki.astype(BF16)
    ki_ref[:, LANES:] = pltpu.roll(ki, IDX_DIM, 1).astype(BF16)
    wt_ref[...] = x.T * (IDX_HEADS ** -0.5)


def _dsa_prep(proj, tabs64, tabs128, gq, gk, gi, tm=256):
    m = proj.shape[0]
    tm = min(tm, m)
    row = lambda n: pl.BlockSpec((tm, n), lambda i: (i, 0))
    vec = pl.BlockSpec((1, LANES), lambda i: (0, 0))
    kvw = SA_KV_HEADS * SA_HEAD_DIM
    gi_pad = jnp.concatenate([gi, jnp.zeros((LANES - IDX_DIM,), F32)])
    return pl.pallas_call(
        _dsa_prep_kernel,
        grid=(m // tm,),
        in_specs=[row(SA_PROJ)] + [row(LANES)] * 6 + [vec] * 3,
        out_specs=[row(SA_HEADS * SA_HEAD_DIM), row(kvw), pl.BlockSpec((kvw, tm), lambda i: (0, i)),
                   row(IDX_HEADS * IDX_DIM), row(2 * LANES), pl.BlockSpec((LANES, tm), lambda i: (0, i))],
        out_shape=[jax.ShapeDtypeStruct((m, SA_HEADS * SA_HEAD_DIM), BF16),
                   jax.ShapeDtypeStruct((m, kvw), BF16),
                   jax.ShapeDtypeStruct((kvw, m), BF16),
                   jax.ShapeDtypeStruct((m, IDX_HEADS * IDX_DIM), BF16),
                   jax.ShapeDtypeStruct((m, 2 * LANES), BF16),
                   jax.ShapeDtypeStruct((LANES, m), F32)],
        compiler_params=_cp("parallel"),
        name="dsa_prep",
    )(proj, *tabs64, *tabs128, gq.reshape(1, LANES), gk.reshape(1, LANES), gi_pad.reshape(1, LANES))


def _dsa_attn_kernel(q_ref, k_ref, vt_ref, qi_ref, ki_ref, wt_ref, o_ref, key_ref, m_ref, acc_ref,
                     *, tq, tk, ksel):
    qb = pl.program_id(1)
    nkv = qb + 1
    row = lax.broadcasted_iota(I32, (tq, tq), 0)
    col = lax.broadcasted_iota(I32, (tq, tq), 1)

    def score_block(j, carry):
        start = pl.multiple_of(j * tq, tq)
        ki_lo = ki_ref[pl.ds(start, tq), :LANES]
        ki_hi = ki_ref[pl.ds(start, tq), LANES:]
        sc = jnp.zeros((tq, tq), F32)
        for hp in range(IDX_HEADS // 2):
            qp = qi_ref[:, hp * LANES:(hp + 1) * LANES]
            w0 = wt_ref[IDX_DIM + 2 * hp:IDX_DIM + 2 * hp + 1, :]
            w1 = wt_ref[IDX_DIM + 2 * hp + 1:IDX_DIM + 2 * hp + 2, :]
            sc = sc + jnp.maximum(_nt(ki_lo, qp), 0.0) * w0 + jnp.maximum(_nt(ki_hi, qp), 0.0) * w1
        bits = pltpu.bitcast(sc, I32)
        key = bits ^ ((bits >> 31) & jnp.int32(0x7FFFFFFF))
        key = jnp.where((j < qb) | (row <= col), key, jnp.int32(INT_MIN))
        key_ref[pl.ds(start, tq), :] = key
        return carry

    lax.fori_loop(0, nkv, score_block, 0)

    @pl.when((nkv * tq) % tk != 0)
    def _():
        key_ref[pl.ds(pl.multiple_of(nkv * tq, tq), tq), :] = jnp.full((tq, tq), INT_MIN, I32)

    def bisect(it, thr):
        cand = thr + jnp.left_shift(jnp.int32(1), 31 - it)

        def count_block(j, cnt):
            start = pl.multiple_of(j * tq, tq)
            ge = jnp.where(key_ref[pl.ds(start, tq), :] >= cand, 1, 0)
            return cnt + jnp.sum(ge.reshape(tq // 8, 8, tq), axis=0)

        cnt = lax.fori_loop(0, nkv, count_block, jnp.zeros((8, tq), I32))
        total = jnp.sum(cnt, axis=0, keepdims=True)
        return jnp.where(total >= ksel, cand, thr)

    thr = lax.fori_loop(0, 32, bisect, jnp.full((1, tq), INT_MIN, I32))
    thr = jnp.maximum(thr, jnp.int32(INT_MIN + 1))

    m_ref[...] = jnp.full(m_ref.shape, NEG_BIG, F32)
    acc_ref[...] = jnp.zeros(acc_ref.shape, F32)
    n_pairs = SA_HEADS // 2
    qs = [jnp.concatenate([q_ref[:, (2 * hp) * LANES:(2 * hp + 1) * LANES],
                           q_ref[:, (2 * hp + 1) * LANES:(2 * hp + 2) * LANES]], axis=0) for hp in range(n_pairs)]

    def attend_block(j, carry):
        start = pl.multiple_of(j * tk, tk)
        bias = jnp.where(key_ref[pl.ds(start, tk), :] >= thr, 0.0, NEG_BIG)
        bias = jnp.concatenate([bias, bias], axis=1)
        ones = jnp.ones((SUM_ROWS, tk), BF16)

        def qk(hp):
            g = (2 * hp) // SA_REP
            return _nt(k_ref[pl.ds(start, tk), g * LANES:(g + 1) * LANES], qs[hp])

        def softmax_pv(hp, s):
            g = (2 * hp) // SA_REP
            vtb = jnp.concatenate([vt_ref[g * LANES:(g + 1) * LANES, pl.ds(start, tk)], ones], axis=0)
            s = s + bias
            m_old = m_ref[hp:hp + 1, :]
            m_new = jnp.maximum(m_old, jnp.max(s, axis=0, keepdims=True))
            alpha = jnp.exp2(m_old - m_new)
            p = jnp.exp2((s - m_new).astype(BF16))
            acc_ref[hp] = alpha * acc_ref[hp] + _dot(vtb, p)
            m_ref[hp:hp + 1, :] = m_new

        s_next = qk(0)
        for hp in range(n_pairs):
            s_cur = s_next
            if hp + 1 < n_pairs:
                s_next = qk(hp + 1)
            softmax_pv(hp, s_cur)
        return carry

    lax.fori_loop(0, (nkv * tq + tk - 1) // tk, attend_block, 0)

    for hp in range(n_pairs):
        acc = acc_ref[hp]
        o = acc[:SA_HEAD_DIM] * (1.0 / acc[SA_HEAD_DIM:SA_HEAD_DIM + 1])
        o_ref[:, (2 * hp) * LANES:(2 * hp + 1) * LANES] = o[:, :tq].T.astype(BF16)
        o_ref[:, (2 * hp + 1) * LANES:(2 * hp + 2) * LANES] = o[:, tq:].T.astype(BF16)


def _dsa_attn(q, k, vt, qi, ki2, wt, bsz, seq, tq=256, tk=512):
    tq = min(tq, seq)
    tk = min(tk, seq)
    assert tk in (tq, 2 * tq)
    nq = seq // tq
    ksel = min(TOPK, seq // 4)
    kvw = SA_KV_HEADS * SA_HEAD_DIM
    qrow = lambda n: pl.BlockSpec((tq, n), lambda b, i: (b * nq + i, 0))
    return pl.pallas_call(
        functools.partial(_dsa_attn_kernel, tq=tq, tk=tk, ksel=ksel),
        grid=(bsz, nq),
        in_specs=[qrow(SA_HEADS * SA_HEAD_DIM),
                  pl.BlockSpec((seq, kvw), lambda b, i: (b, 0)),
                  pl.BlockSpec((kvw, seq), lambda b, i: (0, b)),
                  qrow(IDX_HEADS * IDX_DIM),
                  pl.BlockSpec((seq, 2 * LANES), lambda b, i: (b, 0)),
                  pl.BlockSpec((LANES, tq), lambda b, i: (0, b * nq + i))],
        out_specs=qrow(SA_HEADS * SA_HEAD_DIM),
        out_shape=jax.ShapeDtypeStruct((bsz * seq, SA_HEADS * SA_HEAD_DIM), BF16),
        scratch_shapes=[pltpu.VMEM((seq, tq), I32), pltpu.VMEM((SA_HEADS // 2, 2 * tq), F32),
                        pltpu.VMEM((SA_HEADS // 2, SA_HEAD_DIM + SUM_ROWS, 2 * tq), F32)],
        compiler_params=_cp("parallel", "arbitrary"),
        name="dsa_attn",
    )(q, k, vt, qi, ki2, wt)


def _diff_attention_layer(x, mod, g1, seq, bsz, tabs64, w_in, w_out, gq, gk, lam_params, subln_g, lam_init):
    shift1, scale1, gate1 = mod
    proj = _norm_proj(x, g1, scale1, shift1, w_in.astype(BF16), seq, tn=1024)
    qk, vt = _da_prep(proj, tabs64, gq, gk)
    heads = _da_attn(qk, vt, lam_params, subln_g, bsz, seq, lam_init)
    return _res_matmul(heads, w_out.astype(BF16), x, gate1, seq)


def _ssd_layer(x, mod, g1, seq, bsz, w_in, conv_w, conv_b, dt_bias, a_log, d_skip, norm_g, w_out):
    shift1, scale1, gate1 = mod
    w_dt = w_in[:, SSD_DT_OFF:].reshape(D_MODEL, SSD_GROUPS, SSD_HPG)
    w_dt = jnp.pad(w_dt, ((0, 0), (0, 0), (0, LANES - SSD_HPG))).reshape(D_MODEL, SSD_GROUPS * LANES)
    w_all = jnp.concatenate([w_in[:, :SSD_DT_OFF], w_dt], axis=1).astype(BF16)
    proj = _norm_proj(x, g1, scale1, shift1, w_all, seq, tn=1024)
    xbc = _ssd_conv(proj, conv_w, conv_b, seq)
    y = _ssd_scan(xbc, proj, dt_bias, a_log, d_skip, norm_g, bsz, seq)
    return _res_matmul(y, w_out.astype(BF16), x, gate1, seq)


def _dsa_layer(x, mod, g1, seq, bsz, tabs64, tabs128, w_in, w_out, gq, gk, gi):
    shift1, scale1, gate1 = mod
    w_pad = jnp.pad(w_in, ((0, 0), (0, SA_PROJ - SA_IN))).astype(BF16)
    proj = _norm_proj(x, g1, scale1, shift1, w_pad, seq, tn=1408)
    q, k, vt, qi, ki2, wt = _dsa_prep(proj, tabs64, tabs128, gq, gk, gi)
    heads = _dsa_attn(q, k, vt, qi, ki2, wt, bsz, seq)
    return _res_matmul(heads, w_out.astype(BF16), x, gate1, seq)


def kernel(x, c, positions, norm1_g, norm2_g, ada_w, ada_b, ffn_w_gate_up, ffn_w_down, da_w_in, da_w_out, da_q_norm_g, da_k_norm_g, da_lambda_q1, da_lambda_k1, da_lambda_q2, da_lambda_k2, da_subln_g, ssd_w_in, ssd_conv_w, ssd_conv_b, ssd_dt_bias, ssd_a_log, ssd_d_skip, ssd_norm_g, ssd_w_out, sa_w_in, sa_w_out, sa_q_norm_g, sa_k_norm_g, sa_idx_k_norm_g):
    bsz, seq, d = x.shape
    depth = norm1_g.shape[0]
    m = bsz * seq
    xf = x.reshape(m, d)
    pos_col = positions.reshape(m, 1).astype(I32)
    tabs64 = _rope_tables(pos_col, 64)
    tabs128 = _rope_tables(pos_col, 128)
    ada = _ada(c, ada_w, ada_b).reshape(depth, bsz, ADA_CHUNKS, 1, d)

    for i in range(depth):
        kind, j = i % N_MIXERS, i // N_MIXERS
        mod1 = tuple(ada[i, :, n] for n in range(3))
        shift2, scale2, gate2 = (ada[i, :, n] for n in range(3, 6))
        if kind == 0:
            lam_init = 0.8 - 0.6 * math.exp(-0.3 * i)
            lam_params = jnp.stack([da_lambda_q1[j], da_lambda_k1[j], da_lambda_q2[j], da_lambda_k2[j]])
            xf = _diff_attention_layer(xf, mod1, norm1_g[i], seq, bsz, tabs64, da_w_in[j], da_w_out[j],
                                       da_q_norm_g[j], da_k_norm_g[j], lam_params, da_subln_g[j], lam_init)
        elif kind == 1:
            xf = _ssd_layer(xf, mod1, norm1_g[i], seq, bsz, ssd_w_in[j], ssd_conv_w[j], ssd_conv_b[j],
                            ssd_dt_bias[j], ssd_a_log[j], ssd_d_skip[j], ssd_norm_g[j], ssd_w_out[j])
        else:
            xf = _dsa_layer(xf, mod1, norm1_g[i], seq, bsz, tabs64, tabs128, sa_w_in[j], sa_w_out[j],
                            sa_q_norm_g[j], sa_k_norm_g[j], sa_idx_k_norm_g[j])
        hid = _norm_ffn_up(xf, norm2_g[i], scale2, shift2, ffn_w_gate_up[i].astype(BF16), seq)
        xf = _res_matmul(hid, ffn_w_down[i].astype(BF16), xf, gate2, seq)
    return xf.reshape(bsz, seq, d)
```

```python
import functools
import math

import jax
import jax.numpy as jnp
from jax import lax
from jax.experimental import pallas as pl
from jax.experimental.pallas import tpu as pltpu

F32 = jnp.float32
BF16 = jnp.bfloat16
I32 = jnp.int32

D_MODEL = 2048
N_MIXERS = 3
ROPE_THETA = 500000.0
NORM_EPS = 1e-6
ADA_CHUNKS = 6
FFN_HIDDEN = 5632
LANES = 128
NEG_BIG = -1e30
INT_MIN = -2 ** 31
LOG2E = 1.4426950408889634
SUM_ROWS = 16

DA_HEADS = 16
DA_QK_DIM = 64
DA_Q_COLS = 2048
DA_IN = 6144

SSD_INNER = 4096
SSD_HEAD_DIM = 64
SSD_HEADS = 64
SSD_GROUPS = 8
SSD_HPG = 8
SSD_STATE = 128
SSD_CONV = 4
SSD_CHUNK = 128
SSD_CONV_DIM = 6144
SSD_GW = SSD_HPG * SSD_HEAD_DIM
SSD_DT_OFF = SSD_INNER + SSD_CONV_DIM
SSD_PROJ = SSD_DT_OFF + SSD_GROUPS * LANES

SA_HEADS = 16
SA_KV_HEADS = 4
SA_HEAD_DIM = 128
SA_REP = 4
IDX_HEADS = 16
IDX_DIM = 64
TOPK = 256
SA_IN = 4176
SA_PROJ = 4224
SA_QI_OFF = 3072
SA_KI_OFF = 4096

VMEM_LIMIT = 56 * 1024 * 1024


def _cp(*sem):
    return pltpu.CompilerParams(dimension_semantics=sem, vmem_limit_bytes=VMEM_LIMIT)


def _nt(a, b):
    return lax.dot_general(a, b, (((1,), (1,)), ((), ())), preferred_element_type=F32)


def _dot(a, b):
    return jnp.dot(a, b, preferred_element_type=F32)


def _silu(x):
    return x * (1.0 / (1.0 + jnp.exp(-x)))


def _rope_table_kernel(pos_ref, pat_ref, c_ref, s_ref):
    ang = pos_ref[...].astype(F32) * pat_ref[0:1, :]
    c_ref[...] = jnp.cos(ang)
    s_ref[...] = jnp.sin(ang) * pat_ref[1:2, :]


def _rope_pattern(head_dim):
    rot = head_dim // 4
    half = rot // 2
    lane = jnp.arange(LANES) % head_dim
    inv_freq = jnp.power(jnp.float32(ROPE_THETA), -(lane % half).astype(F32) / half)
    freq = jnp.where(lane < rot, inv_freq, 0.0)
    sign = jnp.where(lane < half, -1.0, jnp.where(lane < rot, 1.0, 0.0))
    pat = jnp.zeros((8, LANES), F32)
    return pat.at[0].set(freq).at[1].set(sign)


def _rope_perm(head_dim):
    rot = head_dim // 4
    half = rot // 2
    src = jnp.arange(LANES)[:, None]
    dst = jnp.arange(LANES)[None, :]
    pos = dst % head_dim
    partner = jnp.where(pos < half, dst + half, dst - half)
    r = ((src == partner) & (pos < rot)).astype(BF16)
    return jnp.concatenate([r, r], axis=0)


def _seg_ones(width):
    lane = jnp.arange(LANES)
    return (lane[:, None] // width == lane[None, :] // width).astype(BF16)


def _rope_tables(pos_col, head_dim):
    m = pos_col.shape[0]
    tm = min(m, 1024)
    out = jax.ShapeDtypeStruct((m, LANES), F32)
    return pl.pallas_call(
        _rope_table_kernel,
        grid=(m // tm,),
        in_specs=[pl.BlockSpec((tm, 1), lambda i: (i, 0)), pl.BlockSpec((8, LANES), lambda i: (0, 0))],
        out_specs=[pl.BlockSpec((tm, LANES), lambda i: (i, 0))] * 2,
        out_shape=[out] * 2,
        compiler_params=_cp("parallel"),
        name="rope_tables",
    )(pos_col, _rope_pattern(head_dim))


def _split_bf16(x, parts):
    out = []
    for _ in range(parts):
        hi = x.astype(BF16)
        out.append(hi)
        x = x - hi.astype(F32)
    return out


def _rope(x, c, s, perm):
    return x * c + _dot(jnp.concatenate(_split_bf16(x, 2), axis=1), perm) * s


def _seg_mean_sq(x, seg_ones, width):
    return _dot((x * x).astype(BF16), seg_ones) * (1.0 / width)


def _ada_kernel(c_ref, w_ref, b_ref, o_ref):
    c = c_ref[...]
    ca = _silu(c).astype(BF16)
    o_ref[...] = _dot(ca, w_ref[...].astype(BF16)) + b_ref[...]


def _ada(c, ada_w, ada_b):
    depth, d, n = ada_w.shape
    bsz = c.shape[0]
    tn = 1024
    return pl.pallas_call(
        _ada_kernel,
        grid=(depth, n // tn),
        in_specs=[pl.BlockSpec((bsz, d), lambda l, j: (0, 0)),
                  pl.BlockSpec((None, d, tn), lambda l, j: (l, 0, j)),
                  pl.BlockSpec((None, 1, tn), lambda l, j: (l, 0, j))],
        out_specs=pl.BlockSpec((None, bsz, tn), lambda l, j: (l, 0, j)),
        out_shape=jax.ShapeDtypeStruct((depth, bsz, n), F32),
        compiler_params=_cp("parallel", "parallel"),
        name="ada",
    )(c, ada_w, ada_b.reshape(depth, 1, n))


def _modulated_norm(x, g, sc, sh):
    ms = jnp.mean(x * x, axis=-1, keepdims=True)
    return (x * lax.rsqrt(ms + NORM_EPS) * g) * (1.0 + sc) + sh


def _norm_proj_kernel(x_ref, g_ref, sc_ref, sh_ref, w_ref, o_ref, h_ref):
    @pl.when(pl.program_id(1) == 0)
    def _():
        h_ref[...] = _modulated_norm(x_ref[...], g_ref[...], sc_ref[...], sh_ref[...]).astype(BF16)

    o_ref[...] = _dot(h_ref[...], w_ref[...]).astype(o_ref.dtype)


def _mod_specs(tm, seq, d):
    row = lambda i, j: (i, 0)
    per_batch = lambda i, j: ((i * tm) // seq, 0, 0)
    return [pl.BlockSpec((tm, d), row),
            pl.BlockSpec((1, d), lambda i, j: (0, 0)),
            pl.BlockSpec((None, 1, d), per_batch),
            pl.BlockSpec((None, 1, d), per_batch)]


def _norm_proj(x, g, scale, shift, w, layer, seq, tn, out_dtype=F32, tm=1024):
    m, d = x.shape
    n = w.shape[2]
    tm = min(tm, seq)
    return pl.pallas_call(
        _norm_proj_kernel,
        grid=(m // tm, n // tn),
        in_specs=_mod_specs(tm, seq, d) + [pl.BlockSpec((None, d, tn), lambda i, j: (layer, 0, j))],
        out_specs=pl.BlockSpec((tm, tn), lambda i, j: (i, j)),
        out_shape=jax.ShapeDtypeStruct((m, n), out_dtype),
        scratch_shapes=[pltpu.VMEM((tm, d), BF16)],
        compiler_params=_cp("parallel", "arbitrary"),
        name="norm_proj",
    )(x, g.reshape(1, d), scale, shift, w)


def _norm_ffn_up_kernel(x_ref, g_ref, sc_ref, sh_ref, wg_ref, wu_ref, o_ref, h_ref):
    @pl.when(pl.program_id(1) == 0)
    def _():
        h_ref[...] = _modulated_norm(x_ref[...], g_ref[...], sc_ref[...], sh_ref[...]).astype(BF16)

    h = h_ref[...]
    gate = _dot(h, wg_ref[...])
    up = _dot(h, wu_ref[...])
    o_ref[...] = (_silu(gate) * up).astype(o_ref.dtype)


def _norm_ffn_up(x, g, scale, shift, w_gate_up, layer, seq, tn=512, tm=1024):
    m, d = x.shape
    hid = w_gate_up.shape[2] // 2
    nj = hid // tn
    tm = min(tm, seq)
    return pl.pallas_call(
        _norm_ffn_up_kernel,
        grid=(m // tm, nj),
        in_specs=_mod_specs(tm, seq, d) + [pl.BlockSpec((None, d, tn), lambda i, j: (layer, 0, j)),
                                           pl.BlockSpec((None, d, tn), lambda i, j: (layer, 0, j + nj))],
        out_specs=pl.BlockSpec((tm, tn), lambda i, j: (i, j)),
        out_shape=jax.ShapeDtypeStruct((m, hid), BF16),
        scratch_shapes=[pltpu.VMEM((tm, d), BF16)],
        compiler_params=_cp("parallel", "arbitrary"),
        name="norm_ffn_up",
    )(x, g.reshape(1, d), scale, shift, w_gate_up, w_gate_up)


def _res_matmul_kernel(a_ref, w_ref, x_ref, gate_ref, o_ref):
    o_ref[...] = x_ref[...] + gate_ref[...] * _dot(a_ref[...], w_ref[...])


def _res_matmul(a, w, layer, x, gate, seq, tn=None, tm=1024):
    m, k = a.shape
    d = w.shape[2]
    tm = min(tm, seq)
    if tn is None:
        tn = 1024 if k <= 4096 else 512
    return pl.pallas_call(
        _res_matmul_kernel,
        grid=(m // tm, d // tn),
        in_specs=[pl.BlockSpec((tm, k), lambda i, j: (i, 0)),
                  pl.BlockSpec((None, k, tn), lambda i, j: (layer, 0, j)),
                  pl.BlockSpec((tm, tn), lambda i, j: (i, j)),
                  pl.BlockSpec((None, 1, tn), lambda i, j: ((i * tm) // seq, 0, j))],
        out_specs=pl.BlockSpec((tm, tn), lambda i, j: (i, j)),
        out_shape=jax.ShapeDtypeStruct((m, d), F32),
        compiler_params=_cp("parallel", "parallel"),
        name="res_matmul",
    )(a, w, x, gate)


def _da_prep_kernel(p_ref, c_ref, s_ref, gq_ref, gk_ref, perm_ref, seg_ref, o_ref, vt_ref):
    c, sn = c_ref[...], s_ref[...]
    perm, seg = perm_ref[...], seg_ref[...]
    n_qk = DA_Q_COLS // LANES
    for blk in range(2 * n_qk):
        x = p_ref[:, blk * LANES:(blk + 1) * LANES]
        ms = _seg_mean_sq(x, seg, DA_QK_DIM)
        gain = gq_ref[...] if blk < n_qk else gk_ref[...]
        y = _rope(x * lax.rsqrt(ms + NORM_EPS) * gain, c, sn, perm)
        if blk < n_qk:
            y = y * (DA_QK_DIM ** -0.5 * LOG2E)
        o_ref[:, blk * LANES:(blk + 1) * LANES] = y.astype(BF16)
    for h in range(DA_HEADS):
        v = p_ref[:, 2 * DA_Q_COLS + h * LANES:2 * DA_Q_COLS + (h + 1) * LANES]
        vt_ref[h * LANES:(h + 1) * LANES, :] = v.T.astype(BF16)


def _da_prep(proj, tabs, gq, gk, tm=256):
    m, n = proj.shape
    tm = min(tm, m)
    tab_spec = pl.BlockSpec((tm, LANES), lambda i: (i, 0))
    vec_spec = pl.BlockSpec((1, LANES), lambda i: (0, 0))
    const = lambda rows: pl.BlockSpec((rows, LANES), lambda i: (0, 0))
    vw = DA_HEADS * LANES
    return pl.pallas_call(
        _da_prep_kernel,
        grid=(m // tm,),
        in_specs=[pl.BlockSpec((tm, n), lambda i: (i, 0)), tab_spec, tab_spec, vec_spec, vec_spec,
                  const(2 * LANES), const(LANES)],
        out_specs=[pl.BlockSpec((tm, 2 * DA_Q_COLS), lambda i: (i, 0)), pl.BlockSpec((vw, tm), lambda i: (0, i))],
        out_shape=[jax.ShapeDtypeStruct((m, 2 * DA_Q_COLS), BF16), jax.ShapeDtypeStruct((vw, m), BF16)],
        compiler_params=_cp("parallel"),
        name="da_prep",
    )(proj, *tabs, jnp.tile(gq, 2).reshape(1, LANES), jnp.tile(gk, 2).reshape(1, LANES),
      _rope_perm(DA_QK_DIM), _seg_ones(DA_QK_DIM))


def _da_attn_kernel(q_ref, k_ref, vt_ref, lam_ref, g_ref, o_ref, m_ref, acc_ref, *, tq, tk, hp, lam_init):
    qb = pl.program_id(2)
    lane = lax.broadcasted_iota(I32, (tq, LANES), 1)
    qs = []
    for e in range(hp):
        q = q_ref[:, e * LANES:(e + 1) * LANES]
        zero = jnp.zeros_like(q)
        qs.append(jnp.concatenate([jnp.where(lane < DA_QK_DIM, q, zero), jnp.where(lane >= DA_QK_DIM, q, zero)], axis=0))
    m_ref[...] = jnp.full(m_ref.shape, NEG_BIG, F32)
    acc_ref[...] = jnp.zeros(acc_ref.shape, F32)

    def step(j, masked):
        start = pl.multiple_of(j * tk, tk)

        def qk(e):
            return _nt(k_ref[pl.ds(start, tk), e * LANES:(e + 1) * LANES], qs[e])

        def softmax_pv(e, s):
            vtb = vt_ref[e * LANES:(e + 1) * LANES, pl.ds(start, tk)]
            vtb = jnp.concatenate([vtb, jnp.ones((SUM_ROWS, tk), BF16)], axis=0)
            if masked:
                key = lax.broadcasted_iota(I32, (tk, 2 * tq), 0) + (j * tk - qb * tq)
                qry = lax.broadcasted_iota(I32, (tk, 2 * tq), 1)
                qry = jnp.where(qry >= tq, qry - tq, qry)
                s = jnp.where(key <= qry, s, NEG_BIG)
            m_old = m_ref[e:e + 1, :]
            m_new = jnp.maximum(m_old, jnp.max(s, axis=0, keepdims=True))
            alpha = jnp.exp2(m_old - m_new)
            p = jnp.exp2((s - m_new).astype(BF16))
            acc_ref[e] = alpha * acc_ref[e] + _dot(vtb, p)
            m_ref[e:e + 1, :] = m_new

        s_next = qk(0)
        for e in range(hp):
            s_cur = s_next
            if e + 1 < hp:
                s_next = qk(e + 1)
            softmax_pv(e, s_cur)

    def body(j, carry):
        step(j, False)
        return carry

    n_full = (qb * tq) // tk
    lax.fori_loop(0, n_full, body, 0)
    step(n_full, True)

    lam_p = lam_ref[...]
    lam = (jnp.exp(jnp.sum(lam_p[0:1] * lam_p[1:2], axis=-1, keepdims=True))
           - jnp.exp(jnp.sum(lam_p[2:3] * lam_p[3:4], axis=-1, keepdims=True)) + lam_init)
    for e in range(hp):
        acc = acc_ref[e]
        o_all = acc[:LANES] * (1.0 / acc[LANES:LANES + 1])
        o = o_all[:, :tq] - lam * o_all[:, tq:]
        ms = jnp.mean(o * o, axis=0, keepdims=True)
        o = o * lax.rsqrt(ms + NORM_EPS) * (g_ref[...] * (1.0 - lam_init))
        o_ref[:, e * LANES:(e + 1) * LANES] = o.T.astype(BF16)


def _da_attn(qk, vt, lam_params, subln_g, bsz, seq, lam_init, tq=256, tk=512, hp=8):
    tq = min(tq, seq)
    tk = min(tk, seq)
    nq = seq // tq
    nh = DA_HEADS // hp
    w = hp * LANES
    return pl.pallas_call(
        functools.partial(_da_attn_kernel, tq=tq, tk=tk, hp=hp, lam_init=lam_init),
        grid=(bsz, nh, nq),
        in_specs=[pl.BlockSpec((tq, w), lambda b, h, i: (b * nq + i, h)),
                  pl.BlockSpec((seq, w), lambda b, h, i: (b, nh + h)),
                  pl.BlockSpec((w, seq), lambda b, h, i: (h, b)),
                  pl.BlockSpec((4, DA_QK_DIM), lambda b, h, i: (0, 0)),
                  pl.BlockSpec((LANES, 1), lambda b, h, i: (0, 0))],
        out_specs=pl.BlockSpec((tq, w), lambda b, h, i: (b * nq + i, h)),
        out_shape=jax.ShapeDtypeStruct((bsz * seq, DA_HEADS * LANES), BF16),
        scratch_shapes=[pltpu.VMEM((hp, 2 * tq), F32), pltpu.VMEM((hp, LANES + SUM_ROWS, 2 * tq), F32)],
        compiler_params=_cp("parallel", "parallel", "arbitrary"),
        name="da_attn",
    )(qk, qk, vt, lam_params, subln_g.reshape(LANES, 1))


def _ssd_conv_kernel(x_ref, halo_ref, w_ref, b_ref, o_ref, xe_ref, *, ts, tiles_per_seq):
    first = (pl.program_id(0) % tiles_per_seq) == 0
    xe_ref[0:8, :] = jnp.where(first, 0.0, halo_ref[...])
    xe_ref[8:, :] = x_ref[...]
    y = b_ref[...] + w_ref[3:4, :] * x_ref[...]
    for tap in range(SSD_CONV - 1):
        y = y + w_ref[tap:tap + 1, :] * xe_ref[pl.ds(5 + tap, ts), :]
    o_ref[...] = _silu(y).astype(BF16)


def _ssd_conv(proj, conv_w, conv_b, seq, ts=512, tc=512):
    m = proj.shape[0]
    ts = min(ts, seq)
    col0 = SSD_INNER // tc
    rb = ts // 8
    return pl.pallas_call(
        functools.partial(_ssd_conv_kernel, ts=ts, tiles_per_seq=seq // ts),
        grid=(m // ts, SSD_CONV_DIM // tc),
        in_specs=[pl.BlockSpec((ts, tc), lambda i, j: (i, col0 + j)),
                  pl.BlockSpec((8, tc), lambda i, j: (jnp.maximum(i * rb - 1, 0), col0 + j)),
                  pl.BlockSpec((SSD_CONV, tc), lambda i, j: (0, j)),
                  pl.BlockSpec((1, tc), lambda i, j: (0, j))],
        out_specs=pl.BlockSpec((ts, tc), lambda i, j: (i, j)),
        out_shape=jax.ShapeDtypeStruct((m, SSD_CONV_DIM), BF16),
        scratch_shapes=[pltpu.VMEM((ts + 8, tc), F32)],
        compiler_params=_cp("parallel", "parallel"),
        name="ssd_conv",
    )(proj, proj, conv_w, conv_b.reshape(1, SSD_CONV_DIM))


def _ssd_scan_kernel(xs_ref, b_ref, c_ref, dt_ref, z_ref, bias_ref, alog_ref, dskip_ref, ng_ref, e_ref,
                     o_ref, state_ref, *, gp):
    q = SSD_CHUNK
    gw = SSD_GW

    @pl.when(pl.program_id(2) == 0)
    def _():
        state_ref[...] = jnp.zeros(state_ref.shape, F32)

    row = lax.broadcasted_iota(I32, (q, q), 0)
    col = lax.broadcasted_iota(I32, (q, q), 1)
    causal = col <= row
    lane_lo = col < SSD_HEAD_DIM
    tri = jnp.where(causal, 1.0, 0.0).astype(BF16)

    for u in range(gp):
        x = dt_ref[:, u * LANES:(u + 1) * LANES] + bias_ref[u]
        dt = jnp.maximum(x, 0.0) + jnp.log1p(jnp.exp(-jnp.abs(x)))
        dta = dt * (-jnp.exp(alog_ref[u]))
        parts = _dot(tri, jnp.concatenate(_split_bf16(dta, 3), axis=1))
        acum = parts[:, :q] + parts[:, q:2 * q] + parts[:, 2 * q:]
        a_last = acum[q - 1:q, :]
        e_acum = jnp.exp(acum)
        w_end = dt * jnp.exp(a_last - acum)
        decay = jnp.broadcast_to(jnp.exp(a_last), (8, q))
        stacked = jnp.concatenate([e_acum, w_end, decay], axis=0)
        expanded = _dot(jnp.concatenate(_split_bf16(stacked, 2), axis=1), e_ref[...])
        e_acum_x = expanded[:q]
        w_end_x = expanded[q:2 * q]
        decay_x = expanded[2 * q:2 * q + 1]

        acum_t = acum.T
        dt_t = dt.T
        bm = b_ref[:, u * LANES:(u + 1) * LANES]
        cm = c_ref[:, u * LANES:(u + 1) * LANES]
        cb = _nt(cm, bm)
        xs = xs_ref[:, u * gw:(u + 1) * gw]
        xs32 = xs.astype(F32)
        state = state_ref[u]
        y_inter = _dot(cm, state.astype(BF16)) * e_acum_x

        y_pairs = []
        for hp in range(SSD_HPG // 2):
            xs_pair = xs[:, hp * LANES:(hp + 1) * LANES]
            ys = []
            for e in range(2):
                h = 2 * hp + e
                seg = acum[:, h:h + 1] - acum_t[h:h + 1, :]
                decay_ts = jnp.exp(jnp.where(causal, seg, NEG_BIG))
                mh = (cb * decay_ts * dt_t[h:h + 1, :]).astype(BF16)
                ys.append(_dot(mh, xs_pair))
            y_pairs.append(jnp.where(lane_lo, ys[0], ys[1]))
        y = jnp.concatenate(y_pairs, axis=1) + y_inter + dskip_ref[u] * xs32

        gated = y * _silu(z_ref[:, u * gw:(u + 1) * gw])
        ms = jnp.mean(gated * gated, axis=-1, keepdims=True)
        o_ref[:, u * gw:(u + 1) * gw] = (gated * lax.rsqrt(ms + NORM_EPS) * ng_ref[u]).astype(BF16)

        bm_t = bm.astype(F32).T.astype(BF16)
        state_ref[u] = state * decay_x + _dot(bm_t, (xs32 * w_end_x).astype(BF16))


def _ssd_scan(xbc, proj, dt_bias, a_log, d_skip, norm_g, bsz, seq, gp=4):
    q = SSD_CHUNK
    nc = seq // q
    g_n = SSD_GROUPS
    gw = SSD_GW
    pad = lambda v: jnp.pad(v.reshape(g_n, 1, SSD_HPG), ((0, 0), (0, 0), (0, LANES - SSD_HPG)))
    expand = (jnp.arange(gw)[None, :] // SSD_HEAD_DIM == jnp.arange(LANES)[:, None]).astype(BF16)
    expand2 = jnp.concatenate([expand, expand], axis=0)
    rows = lambda b, g, c: b * nc + c
    grp = lambda b, g, c: (g, 0, 0)
    bw, cw = gp * LANES, gp * gw
    return pl.pallas_call(
        functools.partial(_ssd_scan_kernel, gp=gp),
        grid=(bsz, g_n // gp, nc),
        in_specs=[pl.BlockSpec((q, cw), lambda b, g, c: (rows(b, g, c), g)),
                  pl.BlockSpec((q, bw), lambda b, g, c: (rows(b, g, c), SSD_INNER // bw + g)),
                  pl.BlockSpec((q, bw), lambda b, g, c: (rows(b, g, c), (SSD_INNER + g_n * LANES) // bw + g)),
                  pl.BlockSpec((q, bw), lambda b, g, c: (rows(b, g, c), SSD_DT_OFF // bw + g)),
                  pl.BlockSpec((q, cw), lambda b, g, c: (rows(b, g, c), g)),
                  pl.BlockSpec((gp, 1, LANES), grp),
                  pl.BlockSpec((gp, 1, LANES), grp),
                  pl.BlockSpec((gp, 1, gw), grp),
                  pl.BlockSpec((gp, 1, gw), grp),
                  pl.BlockSpec((2 * LANES, gw), lambda b, g, c: (0, 0))],
        out_specs=pl.BlockSpec((q, cw), lambda b, g, c: (rows(b, g, c), g)),
        out_shape=jax.ShapeDtypeStruct((bsz * seq, SSD_INNER), BF16),
        scratch_shapes=[pltpu.VMEM((gp, SSD_STATE, gw), F32)],
        compiler_params=_cp("parallel", "parallel", "arbitrary"),
        name="ssd_scan",
    )(xbc, xbc, xbc, proj, proj, pad(dt_bias), pad(a_log),
      jnp.repeat(d_skip, SSD_HEAD_DIM).reshape(g_n, 1, gw), norm_g.reshape(g_n, 1, gw), expand2)


def _dsa_prep_kernel(p_ref, c64_ref, s64_ref, c128_ref, s128_ref, gq_ref, gk_ref, gi_ref, perm64_ref, perm128_ref,
                     seg_ref, q_ref, k_ref, vt_ref, qi_ref, ki_ref, wt_ref):
    tm = p_ref.shape[0]
    c64, s64 = c64_ref[...], s64_ref[...]
    c128, s128 = c128_ref[...], s128_ref[...]
    perm64, perm128, seg = perm64_ref[...], perm128_ref[...], seg_ref[...]

    def normed(x, gain):
        ms = _seg_mean_sq(x, seg, SA_HEAD_DIM)
        return _rope(x * lax.rsqrt(ms + NORM_EPS) * gain, c128, s128, perm128)

    for h in range(SA_HEADS):
        x = p_ref[:, h * LANES:(h + 1) * LANES]
        q_ref[:, h * LANES:(h + 1) * LANES] = (normed(x, gq_ref[...]) * (SA_HEAD_DIM ** -0.5 * LOG2E)).astype(BF16)
    for g in range(SA_KV_HEADS):
        x = p_ref[:, (SA_HEADS + g) * LANES:(SA_HEADS + g + 1) * LANES]
        k_ref[:, g * LANES:(g + 1) * LANES] = normed(x, gk_ref[...]).astype(BF16)
        v = p_ref[:, (SA_HEADS + SA_KV_HEADS + g) * LANES:(SA_HEADS + SA_KV_HEADS + g + 1) * LANES]
        vt_ref[g * LANES:(g + 1) * LANES, :] = v.T.astype(BF16)
    for blk in range(IDX_HEADS * IDX_DIM // LANES):
        x = p_ref[:, SA_QI_OFF + blk * LANES:SA_QI_OFF + (blk + 1) * LANES]
        qi_ref[:, blk * LANES:(blk + 1) * LANES] = (_rope(x, c64, s64, perm64) * (IDX_DIM ** -0.5)).astype(BF16)
    x = p_ref[:, SA_KI_OFF:SA_KI_OFF + LANES]
    lo = lax.broadcasted_iota(I32, (tm, LANES), 1) < IDX_DIM
    xk = jnp.where(lo, x, 0.0)
    ms = jnp.sum(xk * xk, axis=-1, keepdims=True) * (1.0 / IDX_DIM)
    ki = _rope(xk * lax.rsqrt(ms + NORM_EPS) * gi_ref[...], c64, s64, perm64)
    ki = jnp.where(lo, ki, 0.0)
    ki_ref[:, :LANES] = ki.astype(BF16)
    ki_ref[:, LANES:] = pltpu.roll(ki, IDX_DIM, 1).astype(BF16)
    wt_ref[...] = x.T * (IDX_HEADS ** -0.5)


def _dsa_prep(proj, tabs64, tabs128, gq, gk, gi, tm=256):
    m = proj.shape[0]
    tm = min(tm, m)
    row = lambda n: pl.BlockSpec((tm, n), lambda i: (i, 0))
    vec = pl.BlockSpec((1, LANES), lambda i: (0, 0))
    const = lambda rows: pl.BlockSpec((rows, LANES), lambda i: (0, 0))
    kvw = SA_KV_HEADS * SA_HEAD_DIM
    gi_pad = jnp.concatenate([gi, jnp.zeros((LANES - IDX_DIM,), F32)])
    return pl.pallas_call(
        _dsa_prep_kernel,
        grid=(m // tm,),
        in_specs=[row(SA_PROJ)] + [row(LANES)] * 4 + [vec] * 3 + [const(2 * LANES), const(2 * LANES), const(LANES)],
        out_specs=[row(SA_HEADS * SA_HEAD_DIM), row(kvw), pl.BlockSpec((kvw, tm), lambda i: (0, i)),
                   row(IDX_HEADS * IDX_DIM), row(2 * LANES), pl.BlockSpec((LANES, tm), lambda i: (0, i))],
        out_shape=[jax.ShapeDtypeStruct((m, SA_HEADS * SA_HEAD_DIM), BF16),
                   jax.ShapeDtypeStruct((m, kvw), BF16),
                   jax.ShapeDtypeStruct((kvw, m), BF16),
                   jax.ShapeDtypeStruct((m, IDX_HEADS * IDX_DIM), BF16),
                   jax.ShapeDtypeStruct((m, 2 * LANES), BF16),
                   jax.ShapeDtypeStruct((LANES, m), F32)],
        compiler_params=_cp("parallel"),
        name="dsa_prep",
    )(proj, *tabs64, *tabs128, gq.reshape(1, LANES), gk.reshape(1, LANES), gi_pad.reshape(1, LANES),
      _rope_perm(IDX_DIM), _rope_perm(SA_HEAD_DIM), _seg_ones(SA_HEAD_DIM))


def _dsa_attn_kernel(q_ref, k_ref, vt_ref, qi_ref, ki_ref, wt_ref, o_ref, key_ref, m_ref, acc_ref,
                     *, tq, tk, ksel):
    qb = pl.program_id(1)
    nkv = qb + 1
    row = lax.broadcasted_iota(I32, (tq, tq), 0)
    col = lax.broadcasted_iota(I32, (tq, tq), 1)

    def score_block(j, carry):
        start = pl.multiple_of(j * tq, tq)
        ki_lo = ki_ref[pl.ds(start, tq), :LANES]
        ki_hi = ki_ref[pl.ds(start, tq), LANES:]
        sc = jnp.zeros((tq, tq), F32)
        for hp in range(IDX_HEADS // 2):
            qp = qi_ref[:, hp * LANES:(hp + 1) * LANES]
            w0 = wt_ref[IDX_DIM + 2 * hp:IDX_DIM + 2 * hp + 1, :]
            w1 = wt_ref[IDX_DIM + 2 * hp + 1:IDX_DIM + 2 * hp + 2, :]
            sc = sc + jnp.maximum(_nt(ki_lo, qp), 0.0) * w0 + jnp.maximum(_nt(ki_hi, qp), 0.0) * w1
        bits = pltpu.bitcast(sc, I32)
        key = bits ^ ((bits >> 31) & jnp.int32(0x7FFFFFFF))
        key = jnp.where((j < qb) | (row <= col), key, jnp.int32(INT_MIN))
        key_ref[pl.ds(start, tq), :] = key
        return carry

    lax.fori_loop(0, nkv, score_block, 0)

    n_blocks = (nkv * tq + tk - 1) // tk

    @pl.when((nkv * tq) % tk != 0)
    def _():
        key_ref[pl.ds(pl.multiple_of(nkv * tq, tq), tq), :] = jnp.full((tq, tq), INT_MIN, I32)

    def bisect(it, thr):
        cand = thr + jnp.left_shift(jnp.int32(1), 31 - it)

        def count_block(j, cnt):
            start = pl.multiple_of(j * tq, tq)
            ge = jnp.where(key_ref[pl.ds(start, tq), :] >= cand, 1, 0)
            return cnt + jnp.sum(ge.reshape(tq // 8, 8, tq), axis=0)

        cnt = lax.fori_loop(0, nkv, count_block, jnp.zeros((8, tq), I32))
        total = jnp.sum(cnt, axis=0, keepdims=True)
        return jnp.where(total >= ksel, cand, thr)

    thr = lax.fori_loop(0, 32, bisect, jnp.full((1, tq), INT_MIN, I32))
    thr = jnp.maximum(thr, jnp.int32(INT_MIN + 1))

    m_ref[...] = jnp.full(m_ref.shape, NEG_BIG, F32)
    acc_ref[...] = jnp.zeros(acc_ref.shape, F32)
    n_pairs = SA_HEADS // 2
    qs = [jnp.concatenate([q_ref[:, (2 * hp) * LANES:(2 * hp + 1) * LANES],
                           q_ref[:, (2 * hp + 1) * LANES:(2 * hp + 2) * LANES]], axis=0) for hp in range(n_pairs)]

    def attend_block(j, carry):
        start = pl.multiple_of(j * tk, tk)
        bias = jnp.where(key_ref[pl.ds(start, tk), :] >= thr, 0.0, NEG_BIG)
        bias = jnp.concatenate([bias, bias], axis=1)
        ones = jnp.ones((SUM_ROWS, tk), BF16)

        def qk(hp):
            g = (2 * hp) // SA_REP
            return _nt(k_ref[pl.ds(start, tk), g * LANES:(g + 1) * LANES], qs[hp])

        def softmax_pv(hp, s):
            g = (2 * hp) // SA_REP
            vtb = jnp.concatenate([vt_ref[g * LANES:(g + 1) * LANES, pl.ds(start, tk)], ones], axis=0)
            s = s + bias
            m_old = m_ref[hp:hp + 1, :]
            m_new = jnp.maximum(m_old, jnp.max(s, axis=0, keepdims=True))
            alpha = jnp.exp2(m_old - m_new)
            p = jnp.exp2((s - m_new).astype(BF16))
            acc_ref[hp] = alpha * acc_ref[hp] + _dot(vtb, p)
            m_ref[hp:hp + 1, :] = m_new

        s_next = qk(0)
        for hp in range(n_pairs):
            s_cur = s_next
            if hp + 1 < n_pairs:
                s_next = qk(hp + 1)
            softmax_pv(hp, s_cur)
        return carry

    lax.fori_loop(0, n_blocks, attend_block, 0)

    for hp in range(n_pairs):
        acc = acc_ref[hp]
        o = acc[:SA_HEAD_DIM] * (1.0 / acc[SA_HEAD_DIM:SA_HEAD_DIM + 1])
        o_ref[:, (2 * hp) * LANES:(2 * hp + 1) * LANES] = o[:, :tq].T.astype(BF16)
        o_ref[:, (2 * hp + 1) * LANES:(2 * hp + 2) * LANES] = o[:, tq:].T.astype(BF16)


def _dsa_attn(q, k, vt, qi, ki2, wt, bsz, seq, tq=256, tk=512):
    tq = min(tq, seq)
    tk = min(tk, seq)
    assert tk in (tq, 2 * tq)
    nq = seq // tq
    ksel = min(TOPK, seq // 4)
    kvw = SA_KV_HEADS * SA_HEAD_DIM
    qrow = lambda n: pl.BlockSpec((tq, n), lambda b, i: (b * nq + i, 0))
    return pl.pallas_call(
        functools.partial(_dsa_attn_kernel, tq=tq, tk=tk, ksel=ksel),
        grid=(bsz, nq),
        in_specs=[qrow(SA_HEADS * SA_HEAD_DIM),
                  pl.BlockSpec((seq, kvw), lambda b, i: (b, 0)),
                  pl.BlockSpec((kvw, seq), lambda b, i: (0, b)),
                  qrow(IDX_HEADS * IDX_DIM),
                  pl.BlockSpec((seq, 2 * LANES), lambda b, i: (b, 0)),
                  pl.BlockSpec((LANES, tq), lambda b, i: (0, b * nq + i))],
        out_specs=qrow(SA_HEADS * SA_HEAD_DIM),
        out_shape=jax.ShapeDtypeStruct((bsz * seq, SA_HEADS * SA_HEAD_DIM), BF16),
        scratch_shapes=[pltpu.VMEM((seq, tq), I32), pltpu.VMEM((SA_HEADS // 2, 2 * tq), F32),
                        pltpu.VMEM((SA_HEADS // 2, SA_HEAD_DIM + SUM_ROWS, 2 * tq), F32)],
        compiler_params=_cp("parallel", "arbitrary"),
        name="dsa_attn",
    )(q, k, vt, qi, ki2, wt)


def _diff_attention_layer(x, mod, g1, seq, bsz, tabs64, w_in, w_out, layer, gq, gk, lam_params, subln_g, lam_init):
    shift1, scale1, gate1 = mod
    proj = _norm_proj(x, g1, scale1, shift1, w_in, layer, seq, tn=1024)
    qk, vt = _da_prep(proj, tabs64, gq, gk)
    heads = _da_attn(qk, vt, lam_params, subln_g, bsz, seq, lam_init)
    return _res_matmul(heads, w_out, layer, x, gate1, seq)


def _ssd_layer(x, mod, g1, seq, bsz, w_in, conv_w, conv_b, dt_bias, a_log, d_skip, norm_g, w_out):
    shift1, scale1, gate1 = mod
    w_dt = w_in[:, SSD_DT_OFF:].reshape(D_MODEL, SSD_GROUPS, SSD_HPG)
    w_dt = jnp.pad(w_dt, ((0, 0), (0, 0), (0, LANES - SSD_HPG))).reshape(D_MODEL, SSD_GROUPS * LANES)
    w_all = jnp.concatenate([w_in[:, :SSD_DT_OFF], w_dt], axis=1).astype(BF16)
    proj = _norm_proj(x, g1, scale1, shift1, w_all[None], 0, seq, tn=1024)
    xbc = _ssd_conv(proj, conv_w, conv_b, seq)
    y = _ssd_scan(xbc, proj, dt_bias, a_log, d_skip, norm_g, bsz, seq)
    return _res_matmul(y, w_out.astype(BF16)[None], 0, x, gate1, seq)


def _dsa_layer(x, mod, g1, seq, bsz, tabs64, tabs128, w_in, w_out, gq, gk, gi):
    shift1, scale1, gate1 = mod
    w_pad = jnp.pad(w_in, ((0, 0), (0, SA_PROJ - SA_IN))).astype(BF16)
    proj = _norm_proj(x, g1, scale1, shift1, w_pad[None], 0, seq, tn=1408)
    q, k, vt, qi, ki2, wt = _dsa_prep(proj, tabs64, tabs128, gq, gk, gi)
    heads = _dsa_attn(q, k, vt, qi, ki2, wt, bsz, seq)
    return _res_matmul(heads, w_out.astype(BF16)[None], 0, x, gate1, seq)


def kernel(x, c, positions, norm1_g, norm2_g, ada_w, ada_b, ffn_w_gate_up, ffn_w_down, da_w_in, da_w_out, da_q_norm_g, da_k_norm_g, da_lambda_q1, da_lambda_k1, da_lambda_q2, da_lambda_k2, da_subln_g, ssd_w_in, ssd_conv_w, ssd_conv_b, ssd_dt_bias, ssd_a_log, ssd_d_skip, ssd_norm_g, ssd_w_out, sa_w_in, sa_w_out, sa_q_norm_g, sa_k_norm_g, sa_idx_k_norm_g):
    bsz, seq, d = x.shape
    depth = norm1_g.shape[0]
    m = bsz * seq
    xf = x.reshape(m, d)
    pos_col = positions.reshape(m, 1).astype(I32)
    tabs64 = _rope_tables(pos_col, 64)
    tabs128 = _rope_tables(pos_col, 128)
    ada = _ada(c, ada_w, ada_b).reshape(depth, bsz, ADA_CHUNKS, 1, d)
    ffn_up_w = ffn_w_gate_up.astype(BF16)
    ffn_down_w = ffn_w_down.astype(BF16)
    da_in_w = da_w_in.astype(BF16)
    da_out_w = da_w_out.astype(BF16)

    for i in range(depth):
        kind, j = i % N_MIXERS, i // N_MIXERS
        mod1 = tuple(ada[i, :, n] for n in range(3))
        shift2, scale2, gate2 = (ada[i, :, n] for n in range(3, 6))
        if kind == 0:
            lam_init = 0.8 - 0.6 * math.exp(-0.3 * i)
            lam_params = jnp.stack([da_lambda_q1[j], da_lambda_k1[j], da_lambda_q2[j], da_lambda_k2[j]])
            xf = _diff_attention_layer(xf, mod1, norm1_g[i], seq, bsz, tabs64, da_in_w, da_out_w, j,
                                       da_q_norm_g[j], da_k_norm_g[j], lam_params, da_subln_g[j], lam_init)
        elif kind == 1:
            xf = _ssd_layer(xf, mod1, norm1_g[i], seq, bsz, ssd_w_in[j], ssd_conv_w[j], ssd_conv_b[j],
                            ssd_dt_bias[j], ssd_a_log[j], ssd_d_skip[j], ssd_norm_g[j], ssd_w_out[j])
        else:
            xf = _dsa_layer(xf, mod1, norm1_g[i], seq, bsz, tabs64, tabs128, sa_w_in[j], sa_w_out[j],
                            sa_q_norm_g[j], sa_k_norm_g[j], sa_idx_k_norm_g[j])
        hid = _norm_ffn_up(xf, norm2_g[i], scale2, shift2, ffn_up_w, i, seq)
        xf = _res_matmul(hid, ffn_down_w, i, xf, gate2, seq)
    return xf.reshape(bsz, seq, d)
```

```python
import functools
import math

import jax
import jax.numpy as jnp
from jax import lax
from jax.experimental import pallas as pl
from jax.experimental.pallas import tpu as pltpu

F32 = jnp.float32
BF16 = jnp.bfloat16
I32 = jnp.int32

D_MODEL = 2048
N_MIXERS = 3
ROPE_THETA = 500000.0
NORM_EPS = 1e-6
ADA_CHUNKS = 6
FFN_HIDDEN = 5632
LANES = 128
NEG_BIG = -1e30
INT_MIN = -2 ** 31
LOG2E = 1.4426950408889634
SUM_ROWS = 16

DA_HEADS = 16
DA_QK_DIM = 64
DA_Q_COLS = 2048
DA_IN = 6144

SSD_INNER = 4096
SSD_HEAD_DIM = 64
SSD_HEADS = 64
SSD_GROUPS = 8
SSD_HPG = 8
SSD_STATE = 128
SSD_CONV = 4
SSD_CHUNK = 128
SSD_CONV_DIM = 6144
SSD_GW = SSD_HPG * SSD_HEAD_DIM
SSD_DT_OFF = SSD_INNER + SSD_CONV_DIM
SSD_PROJ = SSD_DT_OFF + SSD_GROUPS * LANES

SA_HEADS = 16
SA_KV_HEADS = 4
SA_HEAD_DIM = 128
SA_REP = 4
IDX_HEADS = 16
IDX_DIM = 64
TOPK = 256
SA_IN = 4176
SA_PROJ = 4224
SA_QI_OFF = 3072
SA_KI_OFF = 4096

VMEM_LIMIT = 56 * 1024 * 1024


def _cp(*sem):
    return pltpu.CompilerParams(dimension_semantics=sem, vmem_limit_bytes=VMEM_LIMIT)


def _nt(a, b):
    return lax.dot_general(a, b, (((1,), (1,)), ((), ())), preferred_element_type=F32)


def _dot(a, b):
    return jnp.dot(a, b, preferred_element_type=F32)


def _silu(x):
    return x * (1.0 / (1.0 + jnp.exp(-x)))


def _rope_table_kernel(pos_ref, pat_ref, c_ref, s_ref):
    ang = pos_ref[...].astype(F32) * pat_ref[0:1, :]
    c_ref[...] = jnp.cos(ang)
    s_ref[...] = jnp.sin(ang) * pat_ref[1:2, :]


def _rope_pattern(head_dim):
    rot = head_dim // 4
    half = rot // 2
    lane = jnp.arange(LANES) % head_dim
    inv_freq = jnp.power(jnp.float32(ROPE_THETA), -(lane % half).astype(F32) / half)
    freq = jnp.where(lane < rot, inv_freq, 0.0)
    sign = jnp.where(lane < half, -1.0, jnp.where(lane < rot, 1.0, 0.0))
    pat = jnp.zeros((8, LANES), F32)
    return pat.at[0].set(freq).at[1].set(sign)


def _rope_perm(head_dim):
    rot = head_dim // 4
    half = rot // 2
    src = jnp.arange(LANES)[:, None]
    dst = jnp.arange(LANES)[None, :]
    pos = dst % head_dim
    partner = jnp.where(pos < half, dst + half, dst - half)
    r = ((src == partner) & (pos < rot)).astype(BF16)
    return jnp.concatenate([r, r], axis=0)


def _seg_ones(width):
    lane = jnp.arange(LANES)
    return (lane[:, None] // width == lane[None, :] // width).astype(BF16)


def _rope_tables(pos_col, head_dim):
    m = pos_col.shape[0]
    tm = min(m, 1024)
    out = jax.ShapeDtypeStruct((m, LANES), F32)
    return pl.pallas_call(
        _rope_table_kernel,
        grid=(m // tm,),
        in_specs=[pl.BlockSpec((tm, 1), lambda i: (i, 0)), pl.BlockSpec((8, LANES), lambda i: (0, 0))],
        out_specs=[pl.BlockSpec((tm, LANES), lambda i: (i, 0))] * 2,
        out_shape=[out] * 2,
        compiler_params=_cp("parallel"),
        name="rope_tables",
    )(pos_col, _rope_pattern(head_dim))


def _split_bf16(x, parts):
    out = []
    for _ in range(parts):
        hi = x.astype(BF16)
        out.append(hi)
        x = x - hi.astype(F32)
    return out


def _rope(x, c, s, perm):
    return x * c + _dot(jnp.concatenate(_split_bf16(x, 2), axis=1), perm) * s


def _seg_mean_sq(x, seg_ones, width):
    return _dot((x * x).astype(BF16), seg_ones) * (1.0 / width)


def _ada_kernel(c_ref, w_ref, b_ref, o_ref):
    c = c_ref[...]
    ca = _silu(c).astype(BF16)
    o_ref[...] = _dot(ca, w_ref[...].astype(BF16)) + b_ref[...]


def _ada(c, ada_w, ada_b):
    depth, d, n = ada_w.shape
    bsz = c.shape[0]
    tn = 1024
    return pl.pallas_call(
        _ada_kernel,
        grid=(depth, n // tn),
        in_specs=[pl.BlockSpec((bsz, d), lambda l, j: (0, 0)),
                  pl.BlockSpec((None, d, tn), lambda l, j: (l, 0, j)),
                  pl.BlockSpec((None, 1, tn), lambda l, j: (l, 0, j))],
        out_specs=pl.BlockSpec((None, bsz, tn), lambda l, j: (l, 0, j)),
        out_shape=jax.ShapeDtypeStruct((depth, bsz, n), F32),
        compiler_params=_cp("parallel", "parallel"),
        name="ada",
    )(c, ada_w, ada_b.reshape(depth, 1, n))


def _modulated_norm(x, g, sc, sh):
    ms = jnp.mean(x * x, axis=-1, keepdims=True)
    return (x * lax.rsqrt(ms + NORM_EPS) * g) * (1.0 + sc) + sh


def _norm_proj_kernel(x_ref, g_ref, sc_ref, sh_ref, w_ref, o_ref, h_ref):
    @pl.when(pl.program_id(1) == 0)
    def _():
        h_ref[...] = _modulated_norm(x_ref[...], g_ref[...], sc_ref[...], sh_ref[...]).astype(BF16)

    o_ref[...] = _dot(h_ref[...], w_ref[...]).astype(o_ref.dtype)


def _mod_specs(tm, seq, d):
    row = lambda i, j: (i, 0)
    per_batch = lambda i, j: ((i * tm) // seq, 0, 0)
    return [pl.BlockSpec((tm, d), row),
            pl.BlockSpec((1, d), lambda i, j: (0, 0)),
            pl.BlockSpec((None, 1, d), per_batch),
            pl.BlockSpec((None, 1, d), per_batch)]


def _norm_proj(x, g, scale, shift, w, layer, seq, tn, out_dtype=F32, tm=1024):
    m, d = x.shape
    n = w.shape[2]
    tm = min(tm, seq)
    return pl.pallas_call(
        _norm_proj_kernel,
        grid=(m // tm, n // tn),
        in_specs=_mod_specs(tm, seq, d) + [pl.BlockSpec((None, d, tn), lambda i, j: (layer, 0, j))],
        out_specs=pl.BlockSpec((tm, tn), lambda i, j: (i, j)),
        out_shape=jax.ShapeDtypeStruct((m, n), out_dtype),
        scratch_shapes=[pltpu.VMEM((tm, d), BF16)],
        compiler_params=_cp("parallel", "arbitrary"),
        name="norm_proj",
    )(x, g.reshape(1, d), scale, shift, w)


def _norm_ffn_up_kernel(x_ref, g_ref, sc_ref, sh_ref, wg_ref, wu_ref, o_ref, h_ref):
    @pl.when(pl.program_id(1) == 0)
    def _():
        h_ref[...] = _modulated_norm(x_ref[...], g_ref[...], sc_ref[...], sh_ref[...]).astype(BF16)

    h = h_ref[...]
    gate = _dot(h, wg_ref[...])
    up = _dot(h, wu_ref[...])
    o_ref[...] = (_silu(gate) * up).astype(o_ref.dtype)


def _norm_ffn_up(x, g, scale, shift, w_gate_up, layer, seq, tn=512, tm=1024):
    m, d = x.shape
    hid = w_gate_up.shape[2] // 2
    nj = hid // tn
    tm = min(tm, seq)
    return pl.pallas_call(
        _norm_ffn_up_kernel,
        grid=(m // tm, nj),
        in_specs=_mod_specs(tm, seq, d) + [pl.BlockSpec((None, d, tn), lambda i, j: (layer, 0, j)),
                                           pl.BlockSpec((None, d, tn), lambda i, j: (layer, 0, j + nj))],
        out_specs=pl.BlockSpec((tm, tn), lambda i, j: (i, j)),
        out_shape=jax.ShapeDtypeStruct((m, hid), BF16),
        scratch_shapes=[pltpu.VMEM((tm, d), BF16)],
        compiler_params=_cp("parallel", "arbitrary"),
        name="norm_ffn_up",
    )(x, g.reshape(1, d), scale, shift, w_gate_up, w_gate_up)


def _res_matmul_kernel(a_ref, w_ref, x_ref, gate_ref, o_ref):
    o_ref[...] = x_ref[...] + gate_ref[...] * _dot(a_ref[...], w_ref[...])


def _res_matmul(a, w, layer, x, gate, seq, tn=None, tm=None):
    m, k = a.shape
    d = w.shape[2]
    if tn is None:
        tn, tm = (d, 512) if k <= 2048 else ((1024, 1024) if k <= 4096 else (512, 1024))
    tm = min(tm, seq)
    return pl.pallas_call(
        _res_matmul_kernel,
        grid=(m // tm, d // tn),
        in_specs=[pl.BlockSpec((tm, k), lambda i, j: (i, 0)),
                  pl.BlockSpec((None, k, tn), lambda i, j: (layer, 0, j)),
                  pl.BlockSpec((tm, tn), lambda i, j: (i, j)),
                  pl.BlockSpec((None, 1, tn), lambda i, j: ((i * tm) // seq, 0, j))],
        out_specs=pl.BlockSpec((tm, tn), lambda i, j: (i, j)),
        out_shape=jax.ShapeDtypeStruct((m, d), F32),
        compiler_params=_cp("parallel", "parallel"),
        name="res_matmul",
    )(a, w, x, gate)


def _da_prep_kernel(p_ref, c_ref, s_ref, gq_ref, gk_ref, perm_ref, seg_ref, o_ref, vt_ref):
    c, sn = c_ref[...], s_ref[...]
    perm, seg = perm_ref[...], seg_ref[...]
    n_qk = DA_Q_COLS // LANES
    for blk in range(2 * n_qk):
        x = p_ref[:, blk * LANES:(blk + 1) * LANES]
        ms = _seg_mean_sq(x, seg, DA_QK_DIM)
        gain = gq_ref[...] if blk < n_qk else gk_ref[...]
        y = _rope(x * lax.rsqrt(ms + NORM_EPS) * gain, c, sn, perm)
        if blk < n_qk:
            y = y * (DA_QK_DIM ** -0.5 * LOG2E)
        o_ref[:, blk * LANES:(blk + 1) * LANES] = y.astype(BF16)
    for h in range(DA_HEADS):
        v = p_ref[:, 2 * DA_Q_COLS + h * LANES:2 * DA_Q_COLS + (h + 1) * LANES]
        vt_ref[h * LANES:(h + 1) * LANES, :] = v.T.astype(BF16)


def _da_prep(proj, tabs, gq, gk, tm=256):
    m, n = proj.shape
    tm = min(tm, m)
    tab_spec = pl.BlockSpec((tm, LANES), lambda i: (i, 0))
    vec_spec = pl.BlockSpec((1, LANES), lambda i: (0, 0))
    const = lambda rows: pl.BlockSpec((rows, LANES), lambda i: (0, 0))
    vw = DA_HEADS * LANES
    return pl.pallas_call(
        _da_prep_kernel,
        grid=(m // tm,),
        in_specs=[pl.BlockSpec((tm, n), lambda i: (i, 0)), tab_spec, tab_spec, vec_spec, vec_spec,
                  const(2 * LANES), const(LANES)],
        out_specs=[pl.BlockSpec((tm, 2 * DA_Q_COLS), lambda i: (i, 0)), pl.BlockSpec((vw, tm), lambda i: (0, i))],
        out_shape=[jax.ShapeDtypeStruct((m, 2 * DA_Q_COLS), BF16), jax.ShapeDtypeStruct((vw, m), BF16)],
        compiler_params=_cp("parallel"),
        name="da_prep",
    )(proj, *tabs, jnp.tile(gq, 2).reshape(1, LANES), jnp.tile(gk, 2).reshape(1, LANES),
      _rope_perm(DA_QK_DIM), _seg_ones(DA_QK_DIM))


def _da_attn_kernel(q_ref, k_ref, vt_ref, lam_ref, g_ref, o_ref, m_ref, acc_ref, *, tq, tk, hp, lam_init):
    qb = pl.program_id(2)
    lane = lax.broadcasted_iota(I32, (tq, LANES), 1)
    qs = []
    for e in range(hp):
        q = q_ref[:, e * LANES:(e + 1) * LANES]
        zero = jnp.zeros_like(q)
        qs.append(jnp.concatenate([jnp.where(lane < DA_QK_DIM, q, zero), jnp.where(lane >= DA_QK_DIM, q, zero)], axis=0))
    m_ref[...] = jnp.full(m_ref.shape, NEG_BIG, F32)
    acc_ref[...] = jnp.zeros(acc_ref.shape, F32)

    def step(start, nk, diag_offset):
        def qk(e):
            return _nt(k_ref[pl.ds(start, nk), e * LANES:(e + 1) * LANES], qs[e])

        def softmax_pv(e, s):
            vtb = vt_ref[e * LANES:(e + 1) * LANES, pl.ds(start, nk)]
            vtb = jnp.concatenate([vtb, jnp.ones((SUM_ROWS, nk), BF16)], axis=0)
            if diag_offset is not None:
                key = lax.broadcasted_iota(I32, (nk, 2 * tq), 0)
                qry = lax.broadcasted_iota(I32, (nk, 2 * tq), 1)
                qry = jnp.where(qry >= tq, qry - tq, qry) + diag_offset
                s = jnp.where(key <= qry, s, NEG_BIG)
            m_old = m_ref[e:e + 1, :]
            m_new = jnp.maximum(m_old, jnp.max(s, axis=0, keepdims=True))
            alpha = jnp.exp2(m_old - m_new)
            p = jnp.exp2((s - m_new).astype(BF16))
            acc_ref[e] = alpha * acc_ref[e] + _dot(vtb, p)
            m_ref[e:e + 1, :] = m_new

        s_next = qk(0)
        for e in range(hp):
            s_cur = s_next
            if e + 1 < hp:
                s_next = qk(e + 1)
            softmax_pv(e, s_cur)

    def body(j, carry):
        step(pl.multiple_of(j * tk, tk), tk, None)
        return carry

    n_full = (qb * tq) // tk
    lax.fori_loop(0, n_full, body, 0)
    if tk == tq:
        step(pl.multiple_of(qb * tq, tq), tq, 0)
    else:
        assert tk == 2 * tq

        @pl.when(qb % 2 == 0)
        def _():
            step(pl.multiple_of(qb * tq, tq), tq, 0)

        @pl.when(qb % 2 == 1)
        def _():
            step(pl.multiple_of((qb - 1) * tq, tk), tk, tq)

    lam_p = lam_ref[...]
    lam = (jnp.exp(jnp.sum(lam_p[0:1] * lam_p[1:2], axis=-1, keepdims=True))
           - jnp.exp(jnp.sum(lam_p[2:3] * lam_p[3:4], axis=-1, keepdims=True)) + lam_init)
    for e in range(hp):
        acc = acc_ref[e]
        o_all = acc[:LANES] * (1.0 / acc[LANES:LANES + 1])
        o = o_all[:, :tq] - lam * o_all[:, tq:]
        ms = jnp.mean(o * o, axis=0, keepdims=True)
        o = o * lax.rsqrt(ms + NORM_EPS) * (g_ref[...] * (1.0 - lam_init))
        o_ref[:, e * LANES:(e + 1) * LANES] = o.T.astype(BF16)


def _da_attn(qk, vt, lam_params, subln_g, bsz, seq, lam_init, tq=256, tk=512, hp=8):
    tq = min(tq, seq)
    tk = min(tk, seq)
    nq = seq // tq
    nh = DA_HEADS // hp
    w = hp * LANES
    return pl.pallas_call(
        functools.partial(_da_attn_kernel, tq=tq, tk=tk, hp=hp, lam_init=lam_init),
        grid=(bsz, nh, nq),
        in_specs=[pl.BlockSpec((tq, w), lambda b, h, i: (b * nq + i, h)),
                  pl.BlockSpec((seq, w), lambda b, h, i: (b, nh + h)),
                  pl.BlockSpec((w, seq), lambda b, h, i: (h, b)),
                  pl.BlockSpec((4, DA_QK_DIM), lambda b, h, i: (0, 0)),
                  pl.BlockSpec((LANES, 1), lambda b, h, i: (0, 0))],
        out_specs=pl.BlockSpec((tq, w), lambda b, h, i: (b * nq + i, h)),
        out_shape=jax.ShapeDtypeStruct((bsz * seq, DA_HEADS * LANES), BF16),
        scratch_shapes=[pltpu.VMEM((hp, 2 * tq), F32), pltpu.VMEM((hp, LANES + SUM_ROWS, 2 * tq), F32)],
        compiler_params=_cp("parallel", "parallel", "arbitrary"),
        name="da_attn",
    )(qk, qk, vt, lam_params, subln_g.reshape(LANES, 1))


def _ssd_conv_kernel(x_ref, halo_ref, w_ref, b_ref, o_ref, *, tiles_per_seq):
    first = (pl.program_id(0) % tiles_per_seq) == 0
    x = x_ref[...]
    xe = jnp.concatenate([jnp.where(first, 0.0, halo_ref[...]), x], axis=0)
    y = b_ref[...] + w_ref[3:4, :] * x
    for tap in range(SSD_CONV - 1):
        delay = SSD_CONV - 1 - tap
        y = y + w_ref[tap:tap + 1, :] * pltpu.roll(xe, delay, 0)[8:]
    o_ref[...] = _silu(y).astype(BF16)


def _ssd_conv(proj, conv_w, conv_b, seq, ts=512, tc=512):
    m = proj.shape[0]
    ts = min(ts, seq)
    col0 = SSD_INNER // tc
    rb = ts // 8
    return pl.pallas_call(
        functools.partial(_ssd_conv_kernel, tiles_per_seq=seq // ts),
        grid=(m // ts, SSD_CONV_DIM // tc),
        in_specs=[pl.BlockSpec((ts, tc), lambda i, j: (i, col0 + j)),
                  pl.BlockSpec((8, tc), lambda i, j: (jnp.maximum(i * rb - 1, 0), col0 + j)),
                  pl.BlockSpec((SSD_CONV, tc), lambda i, j: (0, j)),
                  pl.BlockSpec((1, tc), lambda i, j: (0, j))],
        out_specs=pl.BlockSpec((ts, tc), lambda i, j: (i, j)),
        out_shape=jax.ShapeDtypeStruct((m, SSD_CONV_DIM), BF16),
        compiler_params=_cp("parallel", "parallel"),
        name="ssd_conv",
    )(proj, proj, conv_w, conv_b.reshape(1, SSD_CONV_DIM))


def _ssd_scan_kernel(xs_ref, b_ref, c_ref, dt_ref, z_ref, bias_ref, alog_ref, dskip_ref, ng_ref, e_ref,
                     o_ref, state_ref, *, gp):
    q = SSD_CHUNK
    gw = SSD_GW

    @pl.when(pl.program_id(2) == 0)
    def _():
        state_ref[...] = jnp.zeros(state_ref.shape, F32)

    row = lax.broadcasted_iota(I32, (q, q), 0)
    col = lax.broadcasted_iota(I32, (q, q), 1)
    causal = col <= row
    lane_lo = col < SSD_HEAD_DIM
    tri = jnp.where(causal, 1.0, 0.0).astype(BF16)

    for u in range(gp):
        x = dt_ref[:, u * LANES:(u + 1) * LANES] + bias_ref[u]
        dt = jnp.maximum(x, 0.0) + jnp.log1p(jnp.exp(-jnp.abs(x)))
        dta = dt * (-jnp.exp(alog_ref[u]))
        parts = _dot(tri, jnp.concatenate(_split_bf16(dta, 3), axis=1))
        acum = parts[:, :q] + parts[:, q:2 * q] + parts[:, 2 * q:]
        a_last = acum[q - 1:q, :]
        e_acum = jnp.exp(acum)
        w_end = dt * jnp.exp(a_last - acum)
        decay = jnp.broadcast_to(jnp.exp(a_last), (8, q))
        stacked = jnp.concatenate([e_acum, w_end, decay], axis=0)
        expanded = _dot(jnp.concatenate(_split_bf16(stacked, 2), axis=1), e_ref[...])
        e_acum_x = expanded[:q]
        w_end_x = expanded[q:2 * q]
        decay_x = expanded[2 * q:2 * q + 1]

        acum_t = acum.T
        dt_t = dt.T
        bm = b_ref[:, u * LANES:(u + 1) * LANES]
        cm = c_ref[:, u * LANES:(u + 1) * LANES]
        cb = _nt(cm, bm)
        xs = xs_ref[:, u * gw:(u + 1) * gw]
        xs32 = xs.astype(F32)
        state = state_ref[u]
        y_inter = _dot(cm, state.astype(BF16)) * e_acum_x

        y_pairs = []
        for hp in range(SSD_HPG // 2):
            xs_pair = xs[:, hp * LANES:(hp + 1) * LANES]
            ys = []
            for e in range(2):
                h = 2 * hp + e
                seg = acum[:, h:h + 1] - acum_t[h:h + 1, :]
                decay_ts = jnp.exp(jnp.where(causal, seg, NEG_BIG))
                mh = (cb * decay_ts * dt_t[h:h + 1, :]).astype(BF16)
                ys.append(_dot(mh, xs_pair))
            y_pairs.append(jnp.where(lane_lo, ys[0], ys[1]))
        y = jnp.concatenate(y_pairs, axis=1) + y_inter + dskip_ref[u] * xs32

        gated = y * _silu(z_ref[:, u * gw:(u + 1) * gw])
        ms = jnp.mean(gated * gated, axis=-1, keepdims=True)
        o_ref[:, u * gw:(u + 1) * gw] = (gated * lax.rsqrt(ms + NORM_EPS) * ng_ref[u]).astype(BF16)

        bm_t = bm.astype(F32).T.astype(BF16)
        state_ref[u] = state * decay_x + _dot(bm_t, (xs32 * w_end_x).astype(BF16))


def _ssd_scan(xbc, proj, dt_bias, a_log, d_skip, norm_g, bsz, seq, gp=4):
    q = SSD_CHUNK
    nc = seq // q
    g_n = SSD_GROUPS
    gw = SSD_GW
    pad = lambda v: jnp.pad(v.reshape(g_n, 1, SSD_HPG), ((0, 0), (0, 0), (0, LANES - SSD_HPG)))
    expand = (jnp.arange(gw)[None, :] // SSD_HEAD_DIM == jnp.arange(LANES)[:, None]).astype(BF16)
    expand2 = jnp.concatenate([expand, expand], axis=0)
    rows = lambda b, g, c: b * nc + c
    grp = lambda b, g, c: (g, 0, 0)
    bw, cw = gp * LANES, gp * gw
    return pl.pallas_call(
        functools.partial(_ssd_scan_kernel, gp=gp),
        grid=(bsz, g_n // gp, nc),
        in_specs=[pl.BlockSpec((q, cw), lambda b, g, c: (rows(b, g, c), g)),
                  pl.BlockSpec((q, bw), lambda b, g, c: (rows(b, g, c), SSD_INNER // bw + g)),
                  pl.BlockSpec((q, bw), lambda b, g, c: (rows(b, g, c), (SSD_INNER + g_n * LANES) // bw + g)),
                  pl.BlockSpec((q, bw), lambda b, g, c: (rows(b, g, c), SSD_DT_OFF // bw + g)),
                  pl.BlockSpec((q, cw), lambda b, g, c: (rows(b, g, c), g)),
                  pl.BlockSpec((gp, 1, LANES), grp),
                  pl.BlockSpec((gp, 1, LANES), grp),
                  pl.BlockSpec((gp, 1, gw), grp),
                  pl.BlockSpec((gp, 1, gw), grp),
                  pl.BlockSpec((2 * LANES, gw), lambda b, g, c: (0, 0))],
        out_specs=pl.BlockSpec((q, cw), lambda b, g, c: (rows(b, g, c), g)),
        out_shape=jax.ShapeDtypeStruct((bsz * seq, SSD_INNER), BF16),
        scratch_shapes=[pltpu.VMEM((gp, SSD_STATE, gw), F32)],
        compiler_params=_cp("parallel", "parallel", "arbitrary"),
        name="ssd_scan",
    )(xbc, xbc, xbc, proj, proj, pad(dt_bias), pad(a_log),
      jnp.repeat(d_skip, SSD_HEAD_DIM).reshape(g_n, 1, gw), norm_g.reshape(g_n, 1, gw), expand2)


def _dsa_prep_kernel(p_ref, c64_ref, s64_ref, c128_ref, s128_ref, gq_ref, gk_ref, gi_ref, perm64_ref, perm128_ref,
                     seg_ref, q_ref, k_ref, vt_ref, qi_ref, ki_ref, wt_ref):
    tm = p_ref.shape[0]
    c64, s64 = c64_ref[...], s64_ref[...]
    c128, s128 = c128_ref[...], s128_ref[...]
    perm64, perm128, seg = perm64_ref[...], perm128_ref[...], seg_ref[...]

    def normed(x, gain):
        ms = _seg_mean_sq(x, seg, SA_HEAD_DIM)
        return _rope(x * lax.rsqrt(ms + NORM_EPS) * gain, c128, s128, perm128)

    for h in range(SA_HEADS):
        x = p_ref[:, h * LANES:(h + 1) * LANES]
        q_ref[:, h * LANES:(h + 1) * LANES] = (normed(x, gq_ref[...]) * (SA_HEAD_DIM ** -0.5 * LOG2E)).astype(BF16)
    for g in range(SA_KV_HEADS):
        x = p_ref[:, (SA_HEADS + g) * LANES:(SA_HEADS + g + 1) * LANES]
        k_ref[:, g * LANES:(g + 1) * LANES] = normed(x, gk_ref[...]).astype(BF16)
        v = p_ref[:, (SA_HEADS + SA_KV_HEADS + g) * LANES:(SA_HEADS + SA_KV_HEADS + g + 1) * LANES]
        vt_ref[g * LANES:(g + 1) * LANES, :] = v.T.astype(BF16)
    for blk in range(IDX_HEADS * IDX_DIM // LANES):
        x = p_ref[:, SA_QI_OFF + blk * LANES:SA_QI_OFF + (blk + 1) * LANES]
        qi_ref[:, blk * LANES:(blk + 1) * LANES] = (_rope(x, c64, s64, perm64) * (IDX_DIM ** -0.5)).astype(BF16)
    x = p_ref[:, SA_KI_OFF:SA_KI_OFF + LANES]
    lo = lax.broadcasted_iota(I32, (tm, LANES), 1) < IDX_DIM
    xk = jnp.where(lo, x, 0.0)
    ms = jnp.sum(xk * xk, axis=-1, keepdims=True) * (1.0 / IDX_DIM)
    ki = _rope(xk * lax.rsqrt(ms + NORM_EPS) * gi_ref[...], c64, s64, perm64)
    ki = jnp.where(lo, ki, 0.0)
    ki_ref[:, :LANES] = ki.astype(BF16)
    ki_ref[:, LANES:] = pltpu.roll(ki, IDX_DIM, 1).astype(BF16)
    wt_ref[...] = x.T * (IDX_HEADS ** -0.5)


def _dsa_prep(proj, tabs64, tabs128, gq, gk, gi, tm=256):
    m = proj.shape[0]
    tm = min(tm, m)
    row = lambda n: pl.BlockSpec((tm, n), lambda i: (i, 0))
    vec = pl.BlockSpec((1, LANES), lambda i: (0, 0))
    const = lambda rows: pl.BlockSpec((rows, LANES), lambda i: (0, 0))
    kvw = SA_KV_HEADS * SA_HEAD_DIM
    gi_pad = jnp.concatenate([gi, jnp.zeros((LANES - IDX_DIM,), F32)])
    return pl.pallas_call(
        _dsa_prep_kernel,
        grid=(m // tm,),
        in_specs=[row(SA_PROJ)] + [row(LANES)] * 4 + [vec] * 3 + [const(2 * LANES), const(2 * LANES), const(LANES)],
        out_specs=[row(SA_HEADS * SA_HEAD_DIM), row(kvw), pl.BlockSpec((kvw, tm), lambda i: (0, i)),
                   row(IDX_HEADS * IDX_DIM), row(2 * LANES), pl.BlockSpec((LANES, tm), lambda i: (0, i))],
        out_shape=[jax.ShapeDtypeStruct((m, SA_HEADS * SA_HEAD_DIM), BF16),
                   jax.ShapeDtypeStruct((m, kvw), BF16),
                   jax.ShapeDtypeStruct((kvw, m), BF16),
                   jax.ShapeDtypeStruct((m, IDX_HEADS * IDX_DIM), BF16),
                   jax.ShapeDtypeStruct((m, 2 * LANES), BF16),
                   jax.ShapeDtypeStruct((LANES, m), F32)],
        compiler_params=_cp("parallel"),
        name="dsa_prep",
    )(proj, *tabs64, *tabs128, gq.reshape(1, LANES), gk.reshape(1, LANES), gi_pad.reshape(1, LANES),
      _rope_perm(IDX_DIM), _rope_perm(SA_HEAD_DIM), _seg_ones(SA_HEAD_DIM))


def _dsa_attn_kernel(q_ref, k_ref, vt_ref, qi_ref, ki_ref, wt_ref, o_ref, key_ref, m_ref, acc_ref,
                     *, tq, tk, ksel):
    qb = pl.program_id(1)
    nkv = qb + 1
    row = lax.broadcasted_iota(I32, (tq, tq), 0)
    col = lax.broadcasted_iota(I32, (tq, tq), 1)

    def score_block(j, carry):
        start = pl.multiple_of(j * tq, tq)
        ki_lo = ki_ref[pl.ds(start, tq), :LANES]
        ki_hi = ki_ref[pl.ds(start, tq), LANES:]
        sc = jnp.zeros((tq, tq), F32)
        for hp in range(IDX_HEADS // 2):
            qp = qi_ref[:, hp * LANES:(hp + 1) * LANES]
            w0 = wt_ref[IDX_DIM + 2 * hp:IDX_DIM + 2 * hp + 1, :]
            w1 = wt_ref[IDX_DIM + 2 * hp + 1:IDX_DIM + 2 * hp + 2, :]
            sc = sc + jnp.maximum(_nt(ki_lo, qp), 0.0) * w0 + jnp.maximum(_nt(ki_hi, qp), 0.0) * w1
        bits = pltpu.bitcast(sc, I32)
        key = bits ^ ((bits >> 31) & jnp.int32(0x7FFFFFFF))
        key = jnp.where((j < qb) | (row <= col), key, jnp.int32(INT_MIN))
        key_ref[pl.ds(start, tq), :] = key
        return carry

    lax.fori_loop(0, nkv, score_block, 0)

    n_blocks = (nkv * tq + tk - 1) // tk

    @pl.when((nkv * tq) % tk != 0)
    def _():
        key_ref[pl.ds(pl.multiple_of(nkv * tq, tq), tq), :] = jnp.full((tq, tq), INT_MIN, I32)

    def bisect(it, thr):
        cand = thr + jnp.left_shift(jnp.int32(1), 31 - it)

        def count_block(j, cnt):
            start = pl.multiple_of(j * tq, tq)
            ge = jnp.where(key_ref[pl.ds(start, tq), :] >= cand, 1, 0)
            return cnt + jnp.sum(ge.reshape(tq // 8, 8, tq), axis=0)

        cnt = lax.fori_loop(0, nkv, count_block, jnp.zeros((8, tq), I32))
        total = jnp.sum(cnt, axis=0, keepdims=True)
        return jnp.where(total >= ksel, cand, thr)

    thr = lax.fori_loop(0, 32, bisect, jnp.full((1, tq), INT_MIN, I32))
    thr = jnp.maximum(thr, jnp.int32(INT_MIN + 1))

    m_ref[...] = jnp.full(m_ref.shape, NEG_BIG, F32)
    acc_ref[...] = jnp.zeros(acc_ref.shape, F32)
    n_pairs = SA_HEADS // 2
    qs = [jnp.concatenate([q_ref[:, (2 * hp) * LANES:(2 * hp + 1) * LANES],
                           q_ref[:, (2 * hp + 1) * LANES:(2 * hp + 2) * LANES]], axis=0) for hp in range(n_pairs)]

    def attend_block(j, carry):
        start = pl.multiple_of(j * tk, tk)
        bias = jnp.where(key_ref[pl.ds(start, tk), :] >= thr, 0.0, NEG_BIG)
        bias = jnp.concatenate([bias, bias], axis=1)
        ones = jnp.ones((SUM_ROWS, tk), BF16)

        def qk(hp):
            g = (2 * hp) // SA_REP
            return _nt(k_ref[pl.ds(start, tk), g * LANES:(g + 1) * LANES], qs[hp])

        def softmax_pv(hp, s):
            g = (2 * hp) // SA_REP
            vtb = jnp.concatenate([vt_ref[g * LANES:(g + 1) * LANES, pl.ds(start, tk)], ones], axis=0)
            s = s + bias
            m_old = m_ref[hp:hp + 1, :]
            m_new = jnp.maximum(m_old, jnp.max(s, axis=0, keepdims=True))
            alpha = jnp.exp2(m_old - m_new)
            p = jnp.exp2((s - m_new).astype(BF16))
            acc_ref[hp] = alpha * acc_ref[hp] + _dot(vtb, p)
            m_ref[hp:hp + 1, :] = m_new

        s_next = qk(0)
        for hp in range(n_pairs):
            s_cur = s_next
            if hp + 1 < n_pairs:
                s_next = qk(hp + 1)
            softmax_pv(hp, s_cur)
        return carry

    lax.fori_loop(0, n_blocks, attend_block, 0)

    for hp in range(n_pairs):
        acc = acc_ref[hp]
        o = acc[:SA_HEAD_DIM] * (1.0 / acc[SA_HEAD_DIM:SA_HEAD_DIM + 1])
        o_ref[:, (2 * hp) * LANES:(2 * hp + 1) * LANES] = o[:, :tq].T.astype(BF16)
        o_ref[:, (2 * hp + 1) * LANES:(2 * hp + 2) * LANES] = o[:, tq:].T.astype(BF16)


def _dsa_attn(q, k, vt, qi, ki2, wt, bsz, seq, tq=256, tk=512):
    tq = min(tq, seq)
    tk = min(tk, seq)
    assert tk in (tq, 2 * tq)
    nq = seq // tq
    ksel = min(TOPK, seq // 4)
    kvw = SA_KV_HEADS * SA_HEAD_DIM
    qrow = lambda n: pl.BlockSpec((tq, n), lambda b, i: (b * nq + i, 0))
    return pl.pallas_call(
        functools.partial(_dsa_attn_kernel, tq=tq, tk=tk, ksel=ksel),
        grid=(bsz, nq),
        in_specs=[qrow(SA_HEADS * SA_HEAD_DIM),
                  pl.BlockSpec((seq, kvw), lambda b, i: (b, 0)),
                  pl.BlockSpec((kvw, seq), lambda b, i: (0, b)),
                  qrow(IDX_HEADS * IDX_DIM),
                  pl.BlockSpec((seq, 2 * LANES), lambda b, i: (b, 0)),
                  pl.BlockSpec((LANES, tq), lambda b, i: (0, b * nq + i))],
        out_specs=qrow(SA_HEADS * SA_HEAD_DIM),
        out_shape=jax.ShapeDtypeStruct((bsz * seq, SA_HEADS * SA_HEAD_DIM), BF16),
        scratch_shapes=[pltpu.VMEM((seq, tq), I32), pltpu.VMEM((SA_HEADS // 2, 2 * tq), F32),
                        pltpu.VMEM((SA_HEADS // 2, SA_HEAD_DIM + SUM_ROWS, 2 * tq), F32)],
        compiler_params=_cp("parallel", "arbitrary"),
        name="dsa_attn",
    )(q, k, vt, qi, ki2, wt)


def _diff_attention_layer(x, mod, g1, seq, bsz, tabs64, w_in, w_out, layer, gq, gk, lam_params, subln_g, lam_init):
    shift1, scale1, gate1 = mod
    proj = _norm_proj(x, g1, scale1, shift1, w_in, layer, seq, tn=1024)
    qk, vt = _da_prep(proj, tabs64, gq, gk)
    heads = _da_attn(qk, vt, lam_params, subln_g, bsz, seq, lam_init)
    return _res_matmul(heads, w_out, layer, x, gate1, seq)


def _ssd_layer(x, mod, g1, seq, bsz, w_in, conv_w, conv_b, dt_bias, a_log, d_skip, norm_g, w_out):
    shift1, scale1, gate1 = mod
    w_dt = w_in[:, SSD_DT_OFF:].reshape(D_MODEL, SSD_GROUPS, SSD_HPG)
    w_dt = jnp.pad(w_dt, ((0, 0), (0, 0), (0, LANES - SSD_HPG))).reshape(D_MODEL, SSD_GROUPS * LANES)
    w_all = jnp.concatenate([w_in[:, :SSD_DT_OFF], w_dt], axis=1).astype(BF16)
    proj = _norm_proj(x, g1, scale1, shift1, w_all[None], 0, seq, tn=1024)
    xbc = _ssd_conv(proj, conv_w, conv_b, seq)
    y = _ssd_scan(xbc, proj, dt_bias, a_log, d_skip, norm_g, bsz, seq)
    return _res_matmul(y, w_out.astype(BF16)[None], 0, x, gate1, seq)


def _dsa_layer(x, mod, g1, seq, bsz, tabs64, tabs128, w_in, w_out, gq, gk, gi):
    shift1, scale1, gate1 = mod
    w_pad = jnp.pad(w_in, ((0, 0), (0, SA_PROJ - SA_IN))).astype(BF16)
    proj = _norm_proj(x, g1, scale1, shift1, w_pad[None], 0, seq, tn=1408)
    q, k, vt, qi, ki2, wt = _dsa_prep(proj, tabs64, tabs128, gq, gk, gi)
    heads = _dsa_attn(q, k, vt, qi, ki2, wt, bsz, seq)
    return _res_matmul(heads, w_out.astype(BF16)[None], 0, x, gate1, seq)


def kernel(x, c, positions, norm1_g, norm2_g, ada_w, ada_b, ffn_w_gate_up, ffn_w_down, da_w_in, da_w_out, da_q_norm_g, da_k_norm_g, da_lambda_q1, da_lambda_k1, da_lambda_q2, da_lambda_k2, da_subln_g, ssd_w_in, ssd_conv_w, ssd_conv_b, ssd_dt_bias, ssd_a_log, ssd_d_skip, ssd_norm_g, ssd_w_out, sa_w_in, sa_w_out, sa_q_norm_g, sa_k_norm_g, sa_idx_k_norm_g):
    bsz, seq, d = x.shape
    depth = norm1_g.shape[0]
    m = bsz * seq
    xf = x.reshape(m, d)
    pos_col = positions.reshape(m, 1).astype(I32)
    tabs64 = _rope_tables(pos_col, 64)
    tabs128 = _rope_tables(pos_col, 128)
    ada = _ada(c, ada_w, ada_b).reshape(depth, bsz, ADA_CHUNKS, 1, d)
    ffn_up_w = ffn_w_gate_up.astype(BF16)
    ffn_down_w = ffn_w_down.astype(BF16)
    da_in_w = da_w_in.astype(BF16)
    da_out_w = da_w_out.astype(BF16)

    for i in range(depth):
        kind, j = i % N_MIXERS, i // N_MIXERS
        mod1 = tuple(ada[i, :, n] for n in range(3))
        shift2, scale2, gate2 = (ada[i, :, n] for n in range(3, 6))
        if kind == 0:
            lam_init = 0.8 - 0.6 * math.exp(-0.3 * i)
            lam_params = jnp.stack([da_lambda_q1[j], da_lambda_k1[j], da_lambda_q2[j], da_lambda_k2[j]])
            xf = _diff_attention_layer(xf, mod1, norm1_g[i], seq, bsz, tabs64, da_in_w, da_out_w, j,
                                       da_q_norm_g[j], da_k_norm_g[j], lam_params, da_subln_g[j], lam_init)
        elif kind == 1:
            xf = _ssd_layer(xf, mod1, norm1_g[i], seq, bsz, ssd_w_in[j], ssd_conv_w[j], ssd_conv_b[j],
                            ssd_dt_bias[j], ssd_a_log[j], ssd_d_skip[j], ssd_norm_g[j], ssd_w_out[j])
        else:
            xf = _dsa_layer(xf, mod1, norm1_g[i], seq, bsz, tabs64, tabs128, sa_w_in[j], sa_w_out[j],
                            sa_q_norm_g[j], sa_k_norm_g[j], sa_idx_k_norm_g[j])
        hid = _norm_ffn_up(xf, norm2_g[i], scale2, shift2, ffn_up_w, i, seq)
        xf = _res_matmul(hid, ffn_down_w, i, xf, gate2, seq)
    return xf.reshape(bsz, seq, d)
```

```python
import functools
import math

import jax
import jax.numpy as jnp
from jax import lax
from jax.experimental import pallas as pl
from jax.experimental.pallas import tpu as pltpu

F32 = jnp.float32
BF16 = jnp.bfloat16
I32 = jnp.int32

D_MODEL = 2048
N_MIXERS = 3
ROPE_THETA = 500000.0
NORM_EPS = 1e-6
ADA_CHUNKS = 6
FFN_HIDDEN = 5632
LANES = 128
NEG_BIG = -1e30
INT_MIN = -2 ** 31
LOG2E = 1.4426950408889634
SUM_ROWS = 16

DA_HEADS = 16
DA_QK_DIM = 64
DA_Q_COLS = 2048
DA_IN = 6144

SSD_INNER = 4096
SSD_HEAD_DIM = 64
SSD_HEADS = 64
SSD_GROUPS = 8
SSD_HPG = 8
SSD_STATE = 128
SSD_CONV = 4
SSD_CHUNK = 128
SSD_CONV_DIM = 6144
SSD_GW = SSD_HPG * SSD_HEAD_DIM
SSD_DT_OFF = SSD_INNER + SSD_CONV_DIM
SSD_PROJ = SSD_DT_OFF + SSD_GROUPS * LANES

SA_HEADS = 16
SA_KV_HEADS = 4
SA_HEAD_DIM = 128
SA_REP = 4
IDX_HEADS = 16
IDX_DIM = 64
TOPK = 256
SA_IN = 4176
SA_PROJ = 4224
SA_QI_OFF = 3072
SA_KI_OFF = 4096

VMEM_LIMIT = 56 * 1024 * 1024


def _cp(*sem):
    return pltpu.CompilerParams(dimension_semantics=sem, vmem_limit_bytes=VMEM_LIMIT)


def _nt(a, b):
    return lax.dot_general(a, b, (((1,), (1,)), ((), ())), preferred_element_type=F32)


def _dot(a, b):
    return jnp.dot(a, b, preferred_element_type=F32)


def _silu(x):
    return x * (1.0 / (1.0 + jnp.exp(-x)))


def _rope_table_kernel(pos_ref, pat_ref, c_ref, s_ref):
    ang = pos_ref[...].astype(F32) * pat_ref[0:1, :]
    c_ref[...] = jnp.cos(ang)
    s_ref[...] = jnp.sin(ang) * pat_ref[1:2, :]


def _rope_pattern(head_dim):
    rot = head_dim // 4
    half = rot // 2
    lane = jnp.arange(LANES) % head_dim
    inv_freq = jnp.power(jnp.float32(ROPE_THETA), -(lane % half).astype(F32) / half)
    freq = jnp.where(lane < rot, inv_freq, 0.0)
    sign = jnp.where(lane < half, -1.0, jnp.where(lane < rot, 1.0, 0.0))
    pat = jnp.zeros((8, LANES), F32)
    return pat.at[0].set(freq).at[1].set(sign)


def _rope_perm(head_dim):
    rot = head_dim // 4
    half = rot // 2
    src = jnp.arange(LANES)[:, None]
    dst = jnp.arange(LANES)[None, :]
    pos = dst % head_dim
    partner = jnp.where(pos < half, dst + half, dst - half)
    r = ((src == partner) & (pos < rot)).astype(BF16)
    return jnp.concatenate([r, r], axis=0)


def _seg_ones(width):
    lane = jnp.arange(LANES)
    return (lane[:, None] // width == lane[None, :] // width).astype(BF16)


def _rope_tables(pos_col, head_dim):
    m = pos_col.shape[0]
    tm = min(m, 1024)
    out = jax.ShapeDtypeStruct((m, LANES), F32)
    return pl.pallas_call(
        _rope_table_kernel,
        grid=(m // tm,),
        in_specs=[pl.BlockSpec((tm, 1), lambda i: (i, 0)), pl.BlockSpec((8, LANES), lambda i: (0, 0))],
        out_specs=[pl.BlockSpec((tm, LANES), lambda i: (i, 0))] * 2,
        out_shape=[out] * 2,
        compiler_params=_cp("parallel"),
        name="rope_tables",
    )(pos_col, _rope_pattern(head_dim))


def _split_bf16(x, parts):
    out = []
    for _ in range(parts):
        hi = x.astype(BF16)
        out.append(hi)
        x = x - hi.astype(F32)
    return out


def _rope(x, c, s, perm):
    return x * c + _dot(jnp.concatenate(_split_bf16(x, 2), axis=1), perm) * s


def _seg_mean_sq(x, seg_ones, width):
    return _dot((x * x).astype(BF16), seg_ones) * (1.0 / width)


def _ada_kernel(c_ref, w_ref, b_ref, o_ref):
    c = c_ref[...]
    ca = _silu(c).astype(BF16)
    o_ref[...] = _dot(ca, w_ref[...].astype(BF16)) + b_ref[...]


def _ada(c, ada_w, ada_b):
    depth, d, n = ada_w.shape
    bsz = c.shape[0]
    tn = 1024
    return pl.pallas_call(
        _ada_kernel,
        grid=(depth, n // tn),
        in_specs=[pl.BlockSpec((bsz, d), lambda l, j: (0, 0)),
                  pl.BlockSpec((None, d, tn), lambda l, j: (l, 0, j)),
                  pl.BlockSpec((None, 1, tn), lambda l, j: (l, 0, j))],
        out_specs=pl.BlockSpec((None, bsz, tn), lambda l, j: (l, 0, j)),
        out_shape=jax.ShapeDtypeStruct((depth, bsz, n), F32),
        compiler_params=_cp("parallel", "parallel"),
        name="ada",
    )(c, ada_w, ada_b.reshape(depth, 1, n))


def _modulated_norm(x, g, sc, sh):
    ms = jnp.mean(x * x, axis=-1, keepdims=True)
    return (x * lax.rsqrt(ms + NORM_EPS) * g) * (1.0 + sc) + sh


def _norm_proj_kernel(x_ref, g_ref, sc_ref, sh_ref, w_ref, o_ref, h_ref):
    @pl.when(pl.program_id(1) == 0)
    def _():
        h_ref[...] = _modulated_norm(x_ref[...], g_ref[...], sc_ref[...], sh_ref[...]).astype(BF16)

    o_ref[...] = _dot(h_ref[...], w_ref[...]).astype(o_ref.dtype)


def _mod_specs(tm, seq, d):
    row = lambda i, j: (i, 0)
    per_batch = lambda i, j: ((i * tm) // seq, 0, 0)
    return [pl.BlockSpec((tm, d), row),
            pl.BlockSpec((1, d), lambda i, j: (0, 0)),
            pl.BlockSpec((None, 1, d), per_batch),
            pl.BlockSpec((None, 1, d), per_batch)]


def _norm_proj(x, g, scale, shift, w, layer, seq, tn, out_dtype=F32, tm=1024):
    m, d = x.shape
    n = w.shape[2]
    tm = min(tm, seq)
    return pl.pallas_call(
        _norm_proj_kernel,
        grid=(m // tm, n // tn),
        in_specs=_mod_specs(tm, seq, d) + [pl.BlockSpec((None, d, tn), lambda i, j: (layer, 0, j))],
        out_specs=pl.BlockSpec((tm, tn), lambda i, j: (i, j)),
        out_shape=jax.ShapeDtypeStruct((m, n), out_dtype),
        scratch_shapes=[pltpu.VMEM((tm, d), BF16)],
        compiler_params=_cp("parallel", "arbitrary"),
        name="norm_proj",
    )(x, g.reshape(1, d), scale, shift, w)


def _norm_ffn_up_kernel(x_ref, g_ref, sc_ref, sh_ref, wg_ref, wu_ref, o_ref, h_ref):
    @pl.when(pl.program_id(1) == 0)
    def _():
        h_ref[...] = _modulated_norm(x_ref[...], g_ref[...], sc_ref[...], sh_ref[...]).astype(BF16)

    h = h_ref[...]
    gate = _dot(h, wg_ref[...])
    up = _dot(h, wu_ref[...])
    o_ref[...] = (_silu(gate) * up).astype(o_ref.dtype)


def _norm_ffn_up(x, g, scale, shift, w_gate_up, layer, seq, tn=512, tm=1024):
    m, d = x.shape
    hid = w_gate_up.shape[2] // 2
    nj = hid // tn
    tm = min(tm, seq)
    return pl.pallas_call(
        _norm_ffn_up_kernel,
        grid=(m // tm, nj),
        in_specs=_mod_specs(tm, seq, d) + [pl.BlockSpec((None, d, tn), lambda i, j: (layer, 0, j)),
                                           pl.BlockSpec((None, d, tn), lambda i, j: (layer, 0, j + nj))],
        out_specs=pl.BlockSpec((tm, tn), lambda i, j: (i, j)),
        out_shape=jax.ShapeDtypeStruct((m, hid), BF16),
        scratch_shapes=[pltpu.VMEM((tm, d), BF16)],
        compiler_params=_cp("parallel", "arbitrary"),
        name="norm_ffn_up",
    )(x, g.reshape(1, d), scale, shift, w_gate_up, w_gate_up)


def _res_matmul_kernel(a_ref, w_ref, x_ref, gate_ref, o_ref):
    o_ref[...] = x_ref[...] + gate_ref[...] * _dot(a_ref[...], w_ref[...])


def _res_matmul(a, w, layer, x, gate, seq, tn=None, tm=None):
    m, k = a.shape
    d = w.shape[2]
    if tn is None:
        tn, tm = (d, 512) if k <= 2048 else ((1024, 1024) if k <= 4096 else (512, 1024))
    tm = min(tm, seq)
    return pl.pallas_call(
        _res_matmul_kernel,
        grid=(m // tm, d // tn),
        in_specs=[pl.BlockSpec((tm, k), lambda i, j: (i, 0)),
                  pl.BlockSpec((None, k, tn), lambda i, j: (layer, 0, j)),
                  pl.BlockSpec((tm, tn), lambda i, j: (i, j)),
                  pl.BlockSpec((None, 1, tn), lambda i, j: ((i * tm) // seq, 0, j))],
        out_specs=pl.BlockSpec((tm, tn), lambda i, j: (i, j)),
        out_shape=jax.ShapeDtypeStruct((m, d), F32),
        compiler_params=_cp("parallel", "parallel"),
        name="res_matmul",
    )(a, w, x, gate)


def _da_prep_kernel(p_ref, c_ref, s_ref, gq_ref, gk_ref, perm_ref, seg_ref, o_ref, vt_ref):
    c, sn = c_ref[...], s_ref[...]
    perm, seg = perm_ref[...], seg_ref[...]
    n_qk = DA_Q_COLS // LANES
    for blk in range(2 * n_qk):
        x = p_ref[:, blk * LANES:(blk + 1) * LANES]
        ms = _seg_mean_sq(x, seg, DA_QK_DIM)
        gain = gq_ref[...] if blk < n_qk else gk_ref[...]
        y = _rope(x * lax.rsqrt(ms + NORM_EPS) * gain, c, sn, perm)
        if blk < n_qk:
            y = y * (DA_QK_DIM ** -0.5 * LOG2E)
        o_ref[:, blk * LANES:(blk + 1) * LANES] = y.astype(BF16)
    for h in range(DA_HEADS):
        v = p_ref[:, 2 * DA_Q_COLS + h * LANES:2 * DA_Q_COLS + (h + 1) * LANES]
        vt_ref[h * LANES:(h + 1) * LANES, :] = v.T.astype(BF16)


def _da_prep(proj, tabs, gq, gk, tm=512):
    m, n = proj.shape
    tm = min(tm, m)
    tab_spec = pl.BlockSpec((tm, LANES), lambda i: (i, 0))
    vec_spec = pl.BlockSpec((1, LANES), lambda i: (0, 0))
    const = lambda rows: pl.BlockSpec((rows, LANES), lambda i: (0, 0))
    vw = DA_HEADS * LANES
    return pl.pallas_call(
        _da_prep_kernel,
        grid=(m // tm,),
        in_specs=[pl.BlockSpec((tm, n), lambda i: (i, 0)), tab_spec, tab_spec, vec_spec, vec_spec,
                  const(2 * LANES), const(LANES)],
        out_specs=[pl.BlockSpec((tm, 2 * DA_Q_COLS), lambda i: (i, 0)), pl.BlockSpec((vw, tm), lambda i: (0, i))],
        out_shape=[jax.ShapeDtypeStruct((m, 2 * DA_Q_COLS), BF16), jax.ShapeDtypeStruct((vw, m), BF16)],
        compiler_params=_cp("parallel"),
        name="da_prep",
    )(proj, *tabs, jnp.tile(gq, 2).reshape(1, LANES), jnp.tile(gk, 2).reshape(1, LANES),
      _rope_perm(DA_QK_DIM), _seg_ones(DA_QK_DIM))


def _da_attn_kernel(q_ref, k_ref, vt_ref, lam_ref, g_ref, o_ref, m_ref, acc_ref, *, tq, tk, hp, lam_init):
    qb = pl.program_id(2)
    lane = lax.broadcasted_iota(I32, (tq, LANES), 1)
    qs = []
    for e in range(hp):
        q = q_ref[:, e * LANES:(e + 1) * LANES]
        zero = jnp.zeros_like(q)
        qs.append(jnp.concatenate([jnp.where(lane < DA_QK_DIM, q, zero), jnp.where(lane >= DA_QK_DIM, q, zero)], axis=0))
    m_ref[...] = jnp.full(m_ref.shape, NEG_BIG, F32)
    acc_ref[...] = jnp.zeros(acc_ref.shape, F32)

    def step(start, nk, diag_offset):
        def qk(e):
            return _nt(k_ref[pl.ds(start, nk), e * LANES:(e + 1) * LANES], qs[e])

        def softmax_pv(e, s):
            vtb = vt_ref[e * LANES:(e + 1) * LANES, pl.ds(start, nk)]
            vtb = jnp.concatenate([vtb, jnp.ones((SUM_ROWS, nk), BF16)], axis=0)
            if diag_offset is not None:
                key = lax.broadcasted_iota(I32, (nk, 2 * tq), 0)
                qry = lax.broadcasted_iota(I32, (nk, 2 * tq), 1)
                qry = jnp.where(qry >= tq, qry - tq, qry) + diag_offset
                s = jnp.where(key <= qry, s, NEG_BIG)
            m_old = m_ref[e:e + 1, :]
            m_new = jnp.maximum(m_old, jnp.max(s, axis=0, keepdims=True))
            alpha = jnp.exp2(m_old - m_new)
            p = jnp.exp2((s - m_new).astype(BF16))
            acc_ref[e] = alpha * acc_ref[e] + _dot(vtb, p)
            m_ref[e:e + 1, :] = m_new

        s_next = qk(0)
        for e in range(hp):
            s_cur = s_next
            if e + 1 < hp:
                s_next = qk(e + 1)
            softmax_pv(e, s_cur)

    def body(j, carry):
        step(pl.multiple_of(j * tk, tk), tk, None)
        return carry

    n_full = (qb * tq) // tk
    lax.fori_loop(0, n_full, body, 0)
    if tk == tq:
        step(pl.multiple_of(qb * tq, tq), tq, 0)
    else:
        assert tk == 2 * tq

        @pl.when(qb % 2 == 0)
        def _():
            step(pl.multiple_of(qb * tq, tq), tq, 0)

        @pl.when(qb % 2 == 1)
        def _():
            step(pl.multiple_of((qb - 1) * tq, tk), tk, tq)

    lam_p = lam_ref[...]
    lam = (jnp.exp(jnp.sum(lam_p[0:1] * lam_p[1:2], axis=-1, keepdims=True))
           - jnp.exp(jnp.sum(lam_p[2:3] * lam_p[3:4], axis=-1, keepdims=True)) + lam_init)
    for e in range(hp):
        acc = acc_ref[e]
        o_all = acc[:LANES] * (1.0 / acc[LANES:LANES + 1])
        o = o_all[:, :tq] - lam * o_all[:, tq:]
        ms = jnp.mean(o * o, axis=0, keepdims=True)
        o = o * lax.rsqrt(ms + NORM_EPS) * (g_ref[...] * (1.0 - lam_init))
        o_ref[:, e * LANES:(e + 1) * LANES] = o.T.astype(BF16)


def _da_attn(qk, vt, lam_params, subln_g, bsz, seq, lam_init, tq=256, tk=512, hp=8):
    tq = min(tq, seq)
    tk = min(tk, seq)
    nq = seq // tq
    nh = DA_HEADS // hp
    w = hp * LANES
    return pl.pallas_call(
        functools.partial(_da_attn_kernel, tq=tq, tk=tk, hp=hp, lam_init=lam_init),
        grid=(bsz, nh, nq),
        in_specs=[pl.BlockSpec((tq, w), lambda b, h, i: (b * nq + i, h)),
                  pl.BlockSpec((seq, w), lambda b, h, i: (b, nh + h)),
                  pl.BlockSpec((w, seq), lambda b, h, i: (h, b)),
                  pl.BlockSpec((4, DA_QK_DIM), lambda b, h, i: (0, 0)),
                  pl.BlockSpec((LANES, 1), lambda b, h, i: (0, 0))],
        out_specs=pl.BlockSpec((tq, w), lambda b, h, i: (b * nq + i, h)),
        out_shape=jax.ShapeDtypeStruct((bsz * seq, DA_HEADS * LANES), BF16),
        scratch_shapes=[pltpu.VMEM((hp, 2 * tq), F32), pltpu.VMEM((hp, LANES + SUM_ROWS, 2 * tq), F32)],
        compiler_params=_cp("parallel", "parallel", "arbitrary"),
        name="da_attn",
    )(qk, qk, vt, lam_params, subln_g.reshape(LANES, 1))


def _ssd_conv_kernel(x_ref, halo_ref, w_ref, b_ref, o_ref, *, tiles_per_seq):
    first = (pl.program_id(0) % tiles_per_seq) == 0
    x = x_ref[...]
    xe = jnp.concatenate([jnp.where(first, 0.0, halo_ref[...]), x], axis=0)
    y = b_ref[...] + w_ref[3:4, :] * x
    for tap in range(SSD_CONV - 1):
        delay = SSD_CONV - 1 - tap
        y = y + w_ref[tap:tap + 1, :] * pltpu.roll(xe, delay, 0)[8:]
    o_ref[...] = _silu(y).astype(BF16)


def _ssd_conv(proj, conv_w, conv_b, seq, ts=1024, tc=1024):
    m = proj.shape[0]
    ts = min(ts, seq)
    col0 = SSD_INNER // tc
    rb = ts // 8
    return pl.pallas_call(
        functools.partial(_ssd_conv_kernel, tiles_per_seq=seq // ts),
        grid=(m // ts, SSD_CONV_DIM // tc),
        in_specs=[pl.BlockSpec((ts, tc), lambda i, j: (i, col0 + j)),
                  pl.BlockSpec((8, tc), lambda i, j: (jnp.maximum(i * rb - 1, 0), col0 + j)),
                  pl.BlockSpec((SSD_CONV, tc), lambda i, j: (0, j)),
                  pl.BlockSpec((1, tc), lambda i, j: (0, j))],
        out_specs=pl.BlockSpec((ts, tc), lambda i, j: (i, j)),
        out_shape=jax.ShapeDtypeStruct((m, SSD_CONV_DIM), BF16),
        compiler_params=_cp("parallel", "parallel"),
        name="ssd_conv",
    )(proj, proj, conv_w, conv_b.reshape(1, SSD_CONV_DIM))


def _ssd_scan_kernel(xs_ref, b_ref, c_ref, dt_ref, z_ref, bias_ref, alog_ref, dskip_ref, ng_ref, e_ref,
                     o_ref, state_ref, *, gp):
    q = SSD_CHUNK
    gw = SSD_GW

    @pl.when(pl.program_id(2) == 0)
    def _():
        state_ref[...] = jnp.zeros(state_ref.shape, F32)

    row = lax.broadcasted_iota(I32, (q, q), 0)
    col = lax.broadcasted_iota(I32, (q, q), 1)
    causal = col <= row
    lane_lo = col < SSD_HEAD_DIM
    tri = jnp.where(causal, 1.0, 0.0).astype(BF16)

    for u in range(gp):
        x = dt_ref[:, u * LANES:(u + 1) * LANES] + bias_ref[u]
        dt = jnp.maximum(x, 0.0) + jnp.log1p(jnp.exp(-jnp.abs(x)))
        dta = dt * (-jnp.exp(alog_ref[u]))
        parts = _dot(tri, jnp.concatenate(_split_bf16(dta, 3), axis=1))
        acum = parts[:, :q] + parts[:, q:2 * q] + parts[:, 2 * q:]
        a_last = acum[q - 1:q, :]
        e_acum = jnp.exp(acum)
        w_end = dt * jnp.exp(a_last - acum)
        decay = jnp.broadcast_to(jnp.exp(a_last), (8, q))
        stacked = jnp.concatenate([e_acum, w_end, decay], axis=0)
        expanded = _dot(jnp.concatenate(_split_bf16(stacked, 2), axis=1), e_ref[...])
        e_acum_x = expanded[:q]
        w_end_x = expanded[q:2 * q]
        decay_x = expanded[2 * q:2 * q + 1]

        acum_t = acum.T
        dt_t = dt.T
        bm = b_ref[:, u * LANES:(u + 1) * LANES]
        cm = c_ref[:, u * LANES:(u + 1) * LANES]
        cb = _nt(cm, bm)
        xs = xs_ref[:, u * gw:(u + 1) * gw]
        xs32 = xs.astype(F32)
        state = state_ref[u]
        y_inter = _dot(cm, state.astype(BF16)) * e_acum_x

        y_pairs = []
        for hp in range(SSD_HPG // 2):
            xs_pair = xs[:, hp * LANES:(hp + 1) * LANES]
            ys = []
            for e in range(2):
                h = 2 * hp + e
                seg = acum[:, h:h + 1] - acum_t[h:h + 1, :]
                decay_ts = jnp.exp(jnp.where(causal, seg, NEG_BIG))
                mh = (cb * decay_ts * dt_t[h:h + 1, :]).astype(BF16)
                ys.append(_dot(mh, xs_pair))
            y_pairs.append(jnp.where(lane_lo, ys[0], ys[1]))
        y = jnp.concatenate(y_pairs, axis=1) + y_inter + dskip_ref[u] * xs32

        gated = y * _silu(z_ref[:, u * gw:(u + 1) * gw])
        ms = jnp.mean(gated * gated, axis=-1, keepdims=True)
        o_ref[:, u * gw:(u + 1) * gw] = (gated * lax.rsqrt(ms + NORM_EPS) * ng_ref[u]).astype(BF16)

        bm_t = bm.astype(F32).T.astype(BF16)
        state_ref[u] = state * decay_x + _dot(bm_t, (xs32 * w_end_x).astype(BF16))


def _ssd_scan(xbc, proj, dt_bias, a_log, d_skip, norm_g, bsz, seq, gp=8):
    q = SSD_CHUNK
    nc = seq // q
    g_n = SSD_GROUPS
    gw = SSD_GW
    pad = lambda v: jnp.pad(v.reshape(g_n, 1, SSD_HPG), ((0, 0), (0, 0), (0, LANES - SSD_HPG)))
    expand = (jnp.arange(gw)[None, :] // SSD_HEAD_DIM == jnp.arange(LANES)[:, None]).astype(BF16)
    expand2 = jnp.concatenate([expand, expand], axis=0)
    rows = lambda b, g, c: b * nc + c
    grp = lambda b, g, c: (g, 0, 0)
    bw, cw = gp * LANES, gp * gw
    return pl.pallas_call(
        functools.partial(_ssd_scan_kernel, gp=gp),
        grid=(bsz, g_n // gp, nc),
        in_specs=[pl.BlockSpec((q, cw), lambda b, g, c: (rows(b, g, c), g)),
                  pl.BlockSpec((q, bw), lambda b, g, c: (rows(b, g, c), SSD_INNER // bw + g)),
                  pl.BlockSpec((q, bw), lambda b, g, c: (rows(b, g, c), (SSD_INNER + g_n * LANES) // bw + g)),
                  pl.BlockSpec((q, bw), lambda b, g, c: (rows(b, g, c), SSD_DT_OFF // bw + g)),
                  pl.BlockSpec((q, cw), lambda b, g, c: (rows(b, g, c), g)),
                  pl.BlockSpec((gp, 1, LANES), grp),
                  pl.BlockSpec((gp, 1, LANES), grp),
                  pl.BlockSpec((gp, 1, gw), grp),
                  pl.BlockSpec((gp, 1, gw), grp),
                  pl.BlockSpec((2 * LANES, gw), lambda b, g, c: (0, 0))],
        out_specs=pl.BlockSpec((q, cw), lambda b, g, c: (rows(b, g, c), g)),
        out_shape=jax.ShapeDtypeStruct((bsz * seq, SSD_INNER), BF16),
        scratch_shapes=[pltpu.VMEM((gp, SSD_STATE, gw), F32)],
        compiler_params=_cp("parallel", "parallel", "arbitrary"),
        name="ssd_scan",
    )(xbc, xbc, xbc, proj, proj, pad(dt_bias), pad(a_log),
      jnp.repeat(d_skip, SSD_HEAD_DIM).reshape(g_n, 1, gw), norm_g.reshape(g_n, 1, gw), expand2)


def _dsa_prep_kernel(p_ref, c64_ref, s64_ref, c128_ref, s128_ref, gq_ref, gk_ref, gi_ref, perm64_ref, perm128_ref,
                     seg_ref, q_ref, k_ref, vt_ref, qi_ref, ki_ref, wt_ref):
    tm = p_ref.shape[0]
    c64, s64 = c64_ref[...], s64_ref[...]
    c128, s128 = c128_ref[...], s128_ref[...]
    perm64, perm128, seg = perm64_ref[...], perm128_ref[...], seg_ref[...]

    def normed(x, gain):
        ms = _seg_mean_sq(x, seg, SA_HEAD_DIM)
        return _rope(x * lax.rsqrt(ms + NORM_EPS) * gain, c128, s128, perm128)

    for h in range(SA_HEADS):
        x = p_ref[:, h * LANES:(h + 1) * LANES]
        q_ref[:, h * LANES:(h + 1) * LANES] = (normed(x, gq_ref[...]) * (SA_HEAD_DIM ** -0.5 * LOG2E)).astype(BF16)
    for g in range(SA_KV_HEADS):
        x = p_ref[:, (SA_HEADS + g) * LANES:(SA_HEADS + g + 1) * LANES]
        k_ref[:, g * LANES:(g + 1) * LANES] = normed(x, gk_ref[...]).astype(BF16)
        v = p_ref[:, (SA_HEADS + SA_KV_HEADS + g) * LANES:(SA_HEADS + SA_KV_HEADS + g + 1) * LANES]
        vt_ref[g * LANES:(g + 1) * LANES, :] = v.T.astype(BF16)
    for blk in range(IDX_HEADS * IDX_DIM // LANES):
        x = p_ref[:, SA_QI_OFF + blk * LANES:SA_QI_OFF + (blk + 1) * LANES]
        qi_ref[:, blk * LANES:(blk + 1) * LANES] = (_rope(x, c64, s64, perm64) * (IDX_DIM ** -0.5)).astype(BF16)
    x = p_ref[:, SA_KI_OFF:SA_KI_OFF + LANES]
    lo = lax.broadcasted_iota(I32, (tm, LANES), 1) < IDX_DIM
    xk = jnp.where(lo, x, 0.0)
    ms = jnp.sum(xk * xk, axis=-1, keepdims=True) * (1.0 / IDX_DIM)
    ki = _rope(xk * lax.rsqrt(ms + NORM_EPS) * gi_ref[...], c64, s64, perm64)
    ki = jnp.where(lo, ki, 0.0)
    ki_ref[:, :LANES] = ki.astype(BF16)
    ki_ref[:, LANES:] = pltpu.roll(ki, IDX_DIM, 1).astype(BF16)
    wt_ref[...] = x.T * (IDX_HEADS ** -0.5)


def _dsa_prep(proj, tabs64, tabs128, gq, gk, gi, tm=512):
    m = proj.shape[0]
    tm = min(tm, m)
    row = lambda n: pl.BlockSpec((tm, n), lambda i: (i, 0))
    vec = pl.BlockSpec((1, LANES), lambda i: (0, 0))
    const = lambda rows: pl.BlockSpec((rows, LANES), lambda i: (0, 0))
    kvw = SA_KV_HEADS * SA_HEAD_DIM
    gi_pad = jnp.concatenate([gi, jnp.zeros((LANES - IDX_DIM,), F32)])
    return pl.pallas_call(
        _dsa_prep_kernel,
        grid=(m // tm,),
        in_specs=[row(SA_PROJ)] + [row(LANES)] * 4 + [vec] * 3 + [const(2 * LANES), const(2 * LANES), const(LANES)],
        out_specs=[row(SA_HEADS * SA_HEAD_DIM), row(kvw), pl.BlockSpec((kvw, tm), lambda i: (0, i)),
                   row(IDX_HEADS * IDX_DIM), row(2 * LANES), pl.BlockSpec((LANES, tm), lambda i: (0, i))],
        out_shape=[jax.ShapeDtypeStruct((m, SA_HEADS * SA_HEAD_DIM), BF16),
                   jax.ShapeDtypeStruct((m, kvw), BF16),
                   jax.ShapeDtypeStruct((kvw, m), BF16),
                   jax.ShapeDtypeStruct((m, IDX_HEADS * IDX_DIM), BF16),
                   jax.ShapeDtypeStruct((m, 2 * LANES), BF16),
                   jax.ShapeDtypeStruct((LANES, m), F32)],
        compiler_params=_cp("parallel"),
        name="dsa_prep",
    )(proj, *tabs64, *tabs128, gq.reshape(1, LANES), gk.reshape(1, LANES), gi_pad.reshape(1, LANES),
      _rope_perm(IDX_DIM), _rope_perm(SA_HEAD_DIM), _seg_ones(SA_HEAD_DIM))


def _dsa_attn_kernel(q_ref, k_ref, vt_ref, qi_ref, ki_ref, wt_ref, o_ref, key_ref, m_ref, acc_ref,
                     *, tq, tk, ksel):
    qb = pl.program_id(1)
    nkv = qb + 1
    row = lax.broadcasted_iota(I32, (tq, tq), 0)
    col = lax.broadcasted_iota(I32, (tq, tq), 1)

    def score_block(j, carry):
        start = pl.multiple_of(j * tq, tq)
        ki_lo = ki_ref[pl.ds(start, tq), :LANES]
        ki_hi = ki_ref[pl.ds(start, tq), LANES:]
        sc = jnp.zeros((tq, tq), F32)
        for hp in range(IDX_HEADS // 2):
            qp = qi_ref[:, hp * LANES:(hp + 1) * LANES]
            w0 = wt_ref[IDX_DIM + 2 * hp:IDX_DIM + 2 * hp + 1, :]
            w1 = wt_ref[IDX_DIM + 2 * hp + 1:IDX_DIM + 2 * hp + 2, :]
            sc = sc + jnp.maximum(_nt(ki_lo, qp), 0.0) * w0 + jnp.maximum(_nt(ki_hi, qp), 0.0) * w1
        bits = pltpu.bitcast(sc, I32)
        key = bits ^ ((bits >> 31) & jnp.int32(0x7FFFFFFF))
        key = jnp.where((j < qb) | (row <= col), key, jnp.int32(INT_MIN))
        key_ref[pl.ds(start, tq), :] = key
        return carry

    lax.fori_loop(0, nkv, score_block, 0)

    n_blocks = (nkv * tq + tk - 1) // tk

    @pl.when((nkv * tq) % tk != 0)
    def _():
        key_ref[pl.ds(pl.multiple_of(nkv * tq, tq), tq), :] = jnp.full((tq, tq), INT_MIN, I32)

    def bisect(it, thr):
        cand = thr + jnp.left_shift(jnp.int32(1), 31 - it)

        def count_block(j, cnt):
            start = pl.multiple_of(j * tq, tq)
            ge = jnp.where(key_ref[pl.ds(start, tq), :] >= cand, 1, 0)
            return cnt + jnp.sum(ge.reshape(tq // 8, 8, tq), axis=0)

        cnt = lax.fori_loop(0, nkv, count_block, jnp.zeros((8, tq), I32))
        total = jnp.sum(cnt, axis=0, keepdims=True)
        return jnp.where(total >= ksel, cand, thr)

    thr = lax.fori_loop(0, 32, bisect, jnp.full((1, tq), INT_MIN, I32))
    thr = jnp.maximum(thr, jnp.int32(INT_MIN + 1))

    m_ref[...] = jnp.full(m_ref.shape, NEG_BIG, F32)
    acc_ref[...] = jnp.zeros(acc_ref.shape, F32)
    n_pairs = SA_HEADS // 2
    qs = [jnp.concatenate([q_ref[:, (2 * hp) * LANES:(2 * hp + 1) * LANES],
                           q_ref[:, (2 * hp + 1) * LANES:(2 * hp + 2) * LANES]], axis=0) for hp in range(n_pairs)]

    def attend_block(j, carry):
        start = pl.multiple_of(j * tk, tk)
        bias = jnp.where(key_ref[pl.ds(start, tk), :] >= thr, 0.0, NEG_BIG)
        bias = jnp.concatenate([bias, bias], axis=1)
        ones = jnp.ones((SUM_ROWS, tk), BF16)

        def qk(hp):
            g = (2 * hp) // SA_REP
            return _nt(k_ref[pl.ds(start, tk), g * LANES:(g + 1) * LANES], qs[hp])

        def softmax_pv(hp, s):
            g = (2 * hp) // SA_REP
            vtb = jnp.concatenate([vt_ref[g * LANES:(g + 1) * LANES, pl.ds(start, tk)], ones], axis=0)
            s = s + bias
            m_old = m_ref[hp:hp + 1, :]
            m_new = jnp.maximum(m_old, jnp.max(s, axis=0, keepdims=True))
            alpha = jnp.exp2(m_old - m_new)
            p = jnp.exp2((s - m_new).astype(BF16))
            acc_ref[hp] = alpha * acc_ref[hp] + _dot(vtb, p)
            m_ref[hp:hp + 1, :] = m_new

        s_next = qk(0)
        for hp in range(n_pairs):
            s_cur = s_next
            if hp + 1 < n_pairs:
                s_next = qk(hp + 1)
            softmax_pv(hp, s_cur)
        return carry

    lax.fori_loop(0, n_blocks, attend_block, 0)

    for hp in range(n_pairs):
        acc = acc_ref[hp]
        o = acc[:SA_HEAD_DIM] * (1.0 / acc[SA_HEAD_DIM:SA_HEAD_DIM + 1])
        o_ref[:, (2 * hp) * LANES:(2 * hp + 1) * LANES] = o[:, :tq].T.astype(BF16)
        o_ref[:, (2 * hp + 1) * LANES:(2 * hp + 2) * LANES] = o[:, tq:].T.astype(BF16)


def _dsa_attn(q, k, vt, qi, ki2, wt, bsz, seq, tq=256, tk=512):
    tq = min(tq, seq)
    tk = min(tk, seq)
    assert tk in (tq, 2 * tq)
    nq = seq // tq
    ksel = min(TOPK, seq // 4)
    kvw = SA_KV_HEADS * SA_HEAD_DIM
    qrow = lambda n: pl.BlockSpec((tq, n), lambda b, i: (b * nq + i, 0))
    return pl.pallas_call(
        functools.partial(_dsa_attn_kernel, tq=tq, tk=tk, ksel=ksel),
        grid=(bsz, nq),
        in_specs=[qrow(SA_HEADS * SA_HEAD_DIM),
                  pl.BlockSpec((seq, kvw), lambda b, i: (b, 0)),
                  pl.BlockSpec((kvw, seq), lambda b, i: (0, b)),
                  qrow(IDX_HEADS * IDX_DIM),
                  pl.BlockSpec((seq, 2 * LANES), lambda b, i: (b, 0)),
                  pl.BlockSpec((LANES, tq), lambda b, i: (0, b * nq + i))],
        out_specs=qrow(SA_HEADS * SA_HEAD_DIM),
        out_shape=jax.ShapeDtypeStruct((bsz * seq, SA_HEADS * SA_HEAD_DIM), BF16),
        scratch_shapes=[pltpu.VMEM((seq, tq), I32), pltpu.VMEM((SA_HEADS // 2, 2 * tq), F32),
                        pltpu.VMEM((SA_HEADS // 2, SA_HEAD_DIM + SUM_ROWS, 2 * tq), F32)],
        compiler_params=_cp("parallel", "arbitrary"),
        name="dsa_attn",
    )(q, k, vt, qi, ki2, wt)


def _diff_attention_layer(x, mod, g1, seq, bsz, tabs64, w_in, w_out, layer, gq, gk, lam_params, subln_g, lam_init):
    shift1, scale1, gate1 = mod
    proj = _norm_proj(x, g1, scale1, shift1, w_in, layer, seq, tn=1024)
    qk, vt = _da_prep(proj, tabs64, gq, gk)
    heads = _da_attn(qk, vt, lam_params, subln_g, bsz, seq, lam_init)
    return _res_matmul(heads, w_out, layer, x, gate1, seq)


def _ssd_layer(x, mod, g1, seq, bsz, w_in, conv_w, conv_b, dt_bias, a_log, d_skip, norm_g, w_out):
    shift1, scale1, gate1 = mod
    w_dt = w_in[:, SSD_DT_OFF:].reshape(D_MODEL, SSD_GROUPS, SSD_HPG)
    w_dt = jnp.pad(w_dt, ((0, 0), (0, 0), (0, LANES - SSD_HPG))).reshape(D_MODEL, SSD_GROUPS * LANES)
    w_all = jnp.concatenate([w_in[:, :SSD_DT_OFF], w_dt], axis=1).astype(BF16)
    proj = _norm_proj(x, g1, scale1, shift1, w_all[None], 0, seq, tn=1024)
    xbc = _ssd_conv(proj, conv_w, conv_b, seq)
    y = _ssd_scan(xbc, proj, dt_bias, a_log, d_skip, norm_g, bsz, seq)
    return _res_matmul(y, w_out.astype(BF16)[None], 0, x, gate1, seq)


def _dsa_layer(x, mod, g1, seq, bsz, tabs64, tabs128, w_in, w_out, gq, gk, gi):
    shift1, scale1, gate1 = mod
    w_pad = jnp.pad(w_in, ((0, 0), (0, SA_PROJ - SA_IN))).astype(BF16)
    proj = _norm_proj(x, g1, scale1, shift1, w_pad[None], 0, seq, tn=1408)
    q, k, vt, qi, ki2, wt = _dsa_prep(proj, tabs64, tabs128, gq, gk, gi)
    heads = _dsa_attn(q, k, vt, qi, ki2, wt, bsz, seq)
    return _res_matmul(heads, w_out.astype(BF16)[None], 0, x, gate1, seq)


def kernel(x, c, positions, norm1_g, norm2_g, ada_w, ada_b, ffn_w_gate_up, ffn_w_down, da_w_in, da_w_out, da_q_norm_g, da_k_norm_g, da_lambda_q1, da_lambda_k1, da_lambda_q2, da_lambda_k2, da_subln_g, ssd_w_in, ssd_conv_w, ssd_conv_b, ssd_dt_bias, ssd_a_log, ssd_d_skip, ssd_norm_g, ssd_w_out, sa_w_in, sa_w_out, sa_q_norm_g, sa_k_norm_g, sa_idx_k_norm_g):
    bsz, seq, d = x.shape
    depth = norm1_g.shape[0]
    m = bsz * seq
    xf = x.reshape(m, d)
    pos_col = positions.reshape(m, 1).astype(I32)
    tabs64 = _rope_tables(pos_col, 64)
    tabs128 = _rope_tables(pos_col, 128)
    ada = _ada(c, ada_w, ada_b).reshape(depth, bsz, ADA_CHUNKS, 1, d)
    ffn_up_w = ffn_w_gate_up.astype(BF16)
    ffn_down_w = ffn_w_down.astype(BF16)
    da_in_w = da_w_in.astype(BF16)
    da_out_w = da_w_out.astype(BF16)

    for i in range(depth):
        kind, j = i % N_MIXERS, i // N_MIXERS
        mod1 = tuple(ada[i, :, n] for n in range(3))
        shift2, scale2, gate2 = (ada[i, :, n] for n in range(3, 6))
        if kind == 0:
            lam_init = 0.8 - 0.6 * math.exp(-0.3 * i)
            lam_params = jnp.stack([da_lambda_q1[j], da_lambda_k1[j], da_lambda_q2[j], da_lambda_k2[j]])
            xf = _diff_attention_layer(xf, mod1, norm1_g[i], seq, bsz, tabs64, da_in_w, da_out_w, j,
                                       da_q_norm_g[j], da_k_norm_g[j], lam_params, da_subln_g[j], lam_init)
        elif kind == 1:
            xf = _ssd_layer(xf, mod1, norm1_g[i], seq, bsz, ssd_w_in[j], ssd_conv_w[j], ssd_conv_b[j],
                            ssd_dt_bias[j], ssd_a_log[j], ssd_d_skip[j], ssd_norm_g[j], ssd_w_out[j])
        else:
            xf = _dsa_layer(xf, mod1, norm1_g[i], seq, bsz, tabs64, tabs128, sa_w_in[j], sa_w_out[j],
                            sa_q_norm_g[j], sa_k_norm_g[j], sa_idx_k_norm_g[j])
        hid = _norm_ffn_up(xf, norm2_g[i], scale2, shift2, ffn_up_w, i, seq)
        xf = _res_matmul(hid, ffn_down_w, i, xf, gate2, seq)
    return xf.reshape(bsz, seq, d)
```
